```python
import math
import jax, jax.numpy as jnp
from jax import lax
import numpy as np

D_MODEL = 1024
BATCH = 16
SEQ = 2048
DEPTH = 1

HEAD_DIM = 64
HEADS_PER_GROUP = 4
ATTN_GROUPS = ((128, 1), (512, 4), (2048, 16))
N_ATTN_HEADS = HEADS_PER_GROUP * len(ATTN_GROUPS)
ATTN_OUT_W = HEADS_PER_GROUP * HEAD_DIM
ROPE_DIM = HEAD_DIM // 4
ROPE_THETA = 500000.0
BLOCK = 128
SSM_CH_PER_GROUP = 16
SSM_GROUPS = 32
SSM_W = SSM_CH_PER_GROUP * SSM_GROUPS
SSM_STATE = 64
D_FF = -(-8 * D_MODEL // (3 * 256)) * 256
QKV_W = 3 * N_ATTN_HEADS * HEAD_DIM
GATE_W = 2 * D_MODEL
IN_W = QKV_W + SSM_W + GATE_W
RMS_EPS = 1e-6
NEG_INF = -1e30

kernel_name = "hybrid_dilated_attn_s5_gated_block"


def rmsnorm(x, g):
    x32 = x.astype(jnp.float32)
    y = x32 * lax.rsqrt(jnp.mean(x32 * x32, axis=-1, keepdims=True) + RMS_EPS)
    return (y * g.astype(jnp.float32)).astype(x.dtype)


def partial_rope(t, pos):
    half = ROPE_DIM // 2
    inv = jnp.power(jnp.float32(ROPE_THETA), -jnp.arange(half, dtype=jnp.float32) * 2.0 / ROPE_DIM)
    ang = pos[:, None] * inv[None, :]
    cos = jnp.cos(ang)[None, :, None, :]
    sin = jnp.sin(ang)[None, :, None, :]
    tr = t[..., :ROPE_DIM].astype(jnp.float32)
    t1, t2 = tr[..., :half], tr[..., half:]
    rot = jnp.concatenate([t1 * cos - t2 * sin, t2 * cos + t1 * sin], axis=-1)
    return jnp.concatenate([rot.astype(t.dtype), t[..., ROPE_DIM:]], axis=-1)


def dilated_group_attention(q, k, v, window, dilation):
    B, S, H, E = q.shape
    span = window // dilation
    L = S // dilation
    nb = -(-L // BLOCK)
    Lp = nb * BLOCK

    def to_blocks(t):
        t = t.reshape(B, L, dilation, H, E).transpose(0, 2, 1, 3, 4)
        t = jnp.pad(t, ((0, 0), (0, 0), (0, Lp - L), (0, 0), (0, 0)))
        return t.reshape(B, dilation, nb, BLOCK, H, E)

    def with_prev(t):
        prev = jnp.pad(t[:, :, :-1], ((0, 0), (0, 0), (1, 0), (0, 0), (0, 0), (0, 0)))
        return jnp.concatenate([prev, t], axis=3)

    qb = to_blocks(q)
    kk = with_prev(to_blocks(k))
    vv = with_prev(to_blocks(v)).astype(jnp.float32)
    s = jnp.einsum('bdnqhe,bdnkhe->bdnhqk', qb, kk).astype(jnp.float32) * (HEAD_DIM ** -0.5)
    qi = jnp.arange(BLOCK)[:, None]
    ki = jnp.arange(2 * BLOCK)[None, :]
    dist = qi + BLOCK - ki
    band = (dist >= 0) & (dist <= span)
    blk = jnp.arange(nb)[:, None, None]
    valid = band[None] & ((blk > 0) | (ki >= BLOCK)[None])
    s = jnp.where(valid[None, None, :, None], s, NEG_INF)
    m = jnp.max(s, axis=-1, keepdims=True)
    p = jnp.exp(s - m)
    l = jnp.sum(p, axis=-1, keepdims=True)
    o = jnp.einsum('bdnhqk,bdnkhe->bdnhqe', p, vv) / l
    lse = (m + jnp.log(l))[..., 0]
    o = o.transpose(0, 1, 2, 4, 3, 5).reshape(B, dilation, Lp, H, E)[:, :, :L]
    o = o.transpose(0, 2, 1, 3, 4).reshape(B, S, H, E)
    lse = lse.transpose(0, 1, 2, 4, 3).reshape(B, dilation, Lp, H)[:, :, :L]
    lse = lse.transpose(0, 2, 1, 3).reshape(B, S, H)
    return o, lse


def s5_branch(u, a_re, a_im, log_dt, b_re, b_im, c_re, c_im, d_skip, w_glu):
    B, S, _ = u.shape
    u32 = u.astype(jnp.float32).reshape(B, S, SSM_GROUPS, SSM_CH_PER_GROUP)
    lr, li = a_re.astype(jnp.float32), a_im.astype(jnp.float32)
    dt = jnp.exp(log_dt.astype(jnp.float32))[:, None]
    mag = jnp.exp(lr * dt)
    ab_re, ab_im = mag * jnp.cos(li * dt), mag * jnp.sin(li * dt)
    den = lr * lr + li * li
    nr, ni = ab_re - 1.0, ab_im
    f_re = (nr * lr + ni * li) / den
    f_im = (ni * lr - nr * li) / den
    br, bi = b_re.astype(jnp.float32), b_im.astype(jnp.float32)
    bb_re = f_re[..., None] * br - f_im[..., None] * bi
    bb_im = f_re[..., None] * bi + f_im[..., None] * br
    bu_re = jnp.einsum('bsgc,gnc->bsgn', u32, bb_re)
    bu_im = jnp.einsum('bsgc,gnc->bsgn', u32, bb_im)
    a_r = jnp.broadcast_to(ab_re, bu_re.shape)
    a_i = jnp.broadcast_to(ab_im, bu_im.shape)

    def combine(e1, e2):
        a1r, a1i, b1r, b1i = e1
        a2r, a2i, b2r, b2i = e2
        return (a2r * a1r - a2i * a1i,
                a2r * a1i + a2i * a1r,
                a2r * b1r - a2i * b1i + b2r,
                a2r * b1i + a2i * b1r + b2i)

    _, _, xr, xi = lax.associative_scan(combine, (a_r, a_i, bu_re, bu_im), axis=1)
    y = (jnp.einsum('bsgn,gcn->bsgc', xr, c_re.astype(jnp.float32))
         - jnp.einsum('bsgn,gcn->bsgc', xi, c_im.astype(jnp.float32))
         + d_skip.astype(jnp.float32) * u32)
    y = jax.nn.gelu(y.reshape(B, S, SSM_W)).astype(u.dtype)
    z = y @ w_glu
    za, zb = z[..., :D_MODEL], z[..., D_MODEL:]
    return za * jax.nn.sigmoid(zb)


def setup_inputs(seed: int = 0) -> dict:
    key = jax.random.key(seed)
    ks = jax.random.split(key, 24)
    f32 = jnp.float32
    nrm = lambda k, shape, scale: jax.random.normal(k, shape, f32) * scale
    x = jax.random.normal(ks[0], (BATCH, SEQ, D_MODEL), f32)
    norm_mix_g = 1.0 + nrm(ks[1], (DEPTH, D_MODEL), 0.05)
    w_in = nrm(ks[2], (DEPTH, D_MODEL, IN_W), D_MODEL ** -0.5)
    n_idx = jnp.arange(SSM_STATE, dtype=f32)
    ssm_a_re = -0.5 * jnp.exp(nrm(ks[3], (DEPTH, SSM_GROUPS, SSM_STATE), 0.05))
    ssm_a_im = math.pi * n_idx + nrm(ks[4], (DEPTH, SSM_GROUPS, SSM_STATE), 0.01)
    ssm_log_dt = jax.random.uniform(ks[5], (DEPTH, SSM_GROUPS), f32, math.log(1e-3), math.log(1e-1))
    ssm_b_re = nrm(ks[6], (DEPTH, SSM_GROUPS, SSM_STATE, SSM_CH_PER_GROUP), (2.0 * SSM_CH_PER_GROUP) ** -0.5)
    ssm_b_im = nrm(ks[7], (DEPTH, SSM_GROUPS, SSM_STATE, SSM_CH_PER_GROUP), (2.0 * SSM_CH_PER_GROUP) ** -0.5)
    ssm_c_re = nrm(ks[8], (DEPTH, SSM_GROUPS, SSM_CH_PER_GROUP, SSM_STATE), SSM_STATE ** -0.5)
    ssm_c_im = nrm(ks[9], (DEPTH, SSM_GROUPS, SSM_CH_PER_GROUP, SSM_STATE), SSM_STATE ** -0.5)
    ssm_d = nrm(ks[10], (DEPTH, SSM_GROUPS, SSM_CH_PER_GROUP), 1.0)
    w_glu = nrm(ks[11], (DEPTH, SSM_W, 2 * D_MODEL), SSM_W ** -0.5)
    w_attn_out = nrm(ks[12], (DEPTH, ATTN_OUT_W, D_MODEL), ATTN_OUT_W ** -0.5)
    w_out = nrm(ks[13], (DEPTH, D_MODEL, D_MODEL), D_MODEL ** -0.5)
    norm_ffn_g = 1.0 + nrm(ks[14], (DEPTH, D_MODEL), 0.05)
    w_ffn_gate = nrm(ks[15], (DEPTH, D_MODEL, D_FF), D_MODEL ** -0.5)
    w_ffn_up = nrm(ks[16], (DEPTH, D_MODEL, D_FF), D_MODEL ** -0.5)
    w_ffn_down = nrm(ks[17], (DEPTH, D_FF, D_MODEL), D_FF ** -0.5)
    norm_final_g = 1.0 + nrm(ks[18], (D_MODEL,), 0.05)
    return {"x": x, "norm_mix_g": norm_mix_g, "w_in": w_in,
            "ssm_a_re": ssm_a_re, "ssm_a_im": ssm_a_im, "ssm_log_dt": ssm_log_dt,
            "ssm_b_re": ssm_b_re, "ssm_b_im": ssm_b_im, "ssm_c_re": ssm_c_re,
            "ssm_c_im": ssm_c_im, "ssm_d": ssm_d, "w_glu": w_glu,
            "w_attn_out": w_attn_out, "w_out": w_out, "norm_ffn_g": norm_ffn_g,
            "w_ffn_gate": w_ffn_gate, "w_ffn_up": w_ffn_up, "w_ffn_down": w_ffn_down,
            "norm_final_g": norm_final_g}


def reference(x, norm_mix_g, w_in, ssm_a_re, ssm_a_im, ssm_log_dt, ssm_b_re, ssm_b_im,
              ssm_c_re, ssm_c_im, ssm_d, w_glu, w_attn_out, w_out, norm_ffn_g,
              w_ffn_gate, w_ffn_up, w_ffn_down, norm_final_g):
    B, S, D = x.shape
    pos = jnp.arange(S, dtype=jnp.float32)
    for layer in range(DEPTH):
        h = rmsnorm(x, norm_mix_g[layer])
        proj = h @ w_in[layer]
        qkv = proj[..., :QKV_W].reshape(B, S, 3, N_ATTN_HEADS, HEAD_DIM)
        u = proj[..., QKV_W:QKV_W + SSM_W]
        gate = jax.nn.sigmoid(proj[..., QKV_W + SSM_W:].astype(jnp.float32)).reshape(B, S, 2, D)
        q = partial_rope(qkv[:, :, 0], pos)
        k = partial_rope(qkv[:, :, 1], pos)
        v = qkv[:, :, 2]
        outs, lses = [], []
        for gi, (window, dilation) in enumerate(ATTN_GROUPS):
            sl = slice(gi * HEADS_PER_GROUP, (gi + 1) * HEADS_PER_GROUP)
            o_g, lse_g = dilated_group_attention(q[:, :, sl], k[:, :, sl], v[:, :, sl], window, dilation)
            outs.append(o_g)
            lses.append(lse_g)
        outs = jnp.stack(outs, axis=0)
        alpha = jax.nn.softmax(jnp.stack(lses, axis=0), axis=0)
        attn = jnp.sum(alpha[..., None] * outs, axis=0).reshape(B, S, ATTN_OUT_W).astype(x.dtype)
        attn_d = attn @ w_attn_out[layer]
        ssm_out = s5_branch(u, ssm_a_re[layer], ssm_a_im[layer], ssm_log_dt[layer],
                            ssm_b_re[layer], ssm_b_im[layer], ssm_c_re[layer], ssm_c_im[layer],
                            ssm_d[layer], w_glu[layer])
        merged = (gate[:, :, 0] * attn_d.astype(jnp.float32)
                  + gate[:, :, 1] * ssm_out.astype(jnp.float32)).astype(x.dtype)
        x = x + merged @ w_out[layer]
        h2 = rmsnorm(x, norm_ffn_g[layer])
        ff = (jax.nn.silu(h2 @ w_ffn_gate[layer]) * (h2 @ w_ffn_up[layer])) @ w_ffn_down[layer]
        x = x + ff
    return rmsnorm(x, norm_final_g)
```

```python
import functools

import jax
import jax.numpy as jnp
from jax import lax
from jax.experimental import pallas as pl
from jax.experimental.pallas import tpu as pltpu

F32 = jnp.float32
BF16 = jnp.bfloat16

D_MODEL = 1024
HEAD_DIM = 64
HEADS_PER_GROUP = 4
ATTN_GROUPS = ((128, 1), (512, 4), (2048, 16))
N_ATTN_HEADS = HEADS_PER_GROUP * len(ATTN_GROUPS)
ATTN_OUT_W = HEADS_PER_GROUP * HEAD_DIM
ROPE_DIM = HEAD_DIM // 4
ROPE_THETA = 500000.0
BLOCK = 128
SSM_CH = 16
SSM_GROUPS = 32
SSM_W = SSM_CH * SSM_GROUPS
SSM_STATE = 64
SSM_LANES = SSM_GROUPS * SSM_STATE
D_FF = 2816
QK_W = 2 * N_ATTN_HEADS * HEAD_DIM
QKV_W = 3 * N_ATTN_HEADS * HEAD_DIM
QKVU_W = QKV_W + SSM_W
RMS_EPS = 1e-6
NEG_INF = -1e30

LANE = 128
VMEM_LIMIT = 56 * 1024 * 1024

TS_INPROJ = 512
TM_MIX = 512
TM_FFN = 512
SSM_TC = 32
SSM_PAIR = 2


def _rmsnorm(x, g):
    return x * lax.rsqrt(jnp.mean(x * x, axis=-1, keepdims=True) + RMS_EPS) * g


def _const_spec(shape):
    return pl.BlockSpec(shape, lambda *_: (0,) * len(shape), pipeline_mode=pl.Buffered(1))


def _inproj_kernel(x_ref, g_ref, w_ref, rc_ref, ra_ref, rb_ref, qkv_ref, u_ref):
    h = _rmsnorm(x_ref[0], g_ref[...]).astype(BF16)
    p = jnp.dot(h, w_ref[...], preferred_element_type=F32)
    rc, ra, rb = rc_ref[...], ra_ref[...], rb_ref[...]
    for c in range(QK_W // LANE):
        t = p[:, c * LANE:(c + 1) * LANE]
        half = ROPE_DIM // 2
        t = t * rc + pltpu.roll(t, LANE - half, 1) * ra + pltpu.roll(t, half, 1) * rb
        qkv_ref[0, :, c * LANE:(c + 1) * LANE] = t.astype(BF16)
    qkv_ref[0, :, QK_W:QKV_W] = p[:, QK_W:QKV_W].astype(BF16)
    u_ref[0] = p[:, QKV_W:QKVU_W]


def _rope_tables(S):
    half = ROPE_DIM // 2
    pos = jnp.arange(S, dtype=F32)
    inv = jnp.power(jnp.float32(ROPE_THETA), -jnp.arange(half, dtype=F32) * 2.0 / ROPE_DIM)
    ang = pos[:, None] * inv[None, :]
    cos, sin = jnp.cos(ang), jnp.sin(ang)
    ones = jnp.ones((S, HEAD_DIM - ROPE_DIM), F32)
    zeros_h = jnp.zeros((S, half), F32)
    zeros_r = jnp.zeros((S, HEAD_DIM - ROPE_DIM), F32)
    per_head_c = jnp.concatenate([cos, cos, ones], axis=1)
    per_head_a = jnp.concatenate([-sin, zeros_h, zeros_r], axis=1)
    per_head_b = jnp.concatenate([zeros_h, sin, zeros_r], axis=1)
    rep = LANE // HEAD_DIM
    return tuple(jnp.tile(t, (1, rep)) for t in (per_head_c, per_head_a, per_head_b))


def _inproj(x, g, w, rope):
    B, S, D = x.shape
    ts = TS_INPROJ
    tab_spec = pl.BlockSpec((ts, LANE), lambda b, i: (i, 0))
    return pl.pallas_call(
        _inproj_kernel,
        grid=(B, S // ts),
        in_specs=[
            pl.BlockSpec((1, ts, D), lambda b, i: (b, i, 0)),
            _const_spec((1, D)),
            _const_spec((D, QKVU_W)),
            tab_spec, tab_spec, tab_spec,
        ],
        out_specs=[
            pl.BlockSpec((1, ts, QKV_W), lambda b, i: (b, i, 0)),
            pl.BlockSpec((1, ts, SSM_W), lambda b, i: (b, i, 0)),
        ],
        out_shape=[
            jax.ShapeDtypeStruct((B, S, QKV_W), BF16),
            jax.ShapeDtypeStruct((B, S, SSM_W), F32),
        ],
        compiler_params=pltpu.CompilerParams(
            dimension_semantics=("arbitrary", "arbitrary"), vmem_limit_bytes=VMEM_LIMIT),
        name="inproj",
    )(x, g, w, *rope)


def _attn_block(q, k, v, mask, lane_head):
    zero = jnp.zeros_like(q)
    qm = jnp.concatenate(
        [jnp.where(lane_head == h, q, zero).astype(BF16) for h in range(HEADS_PER_GROUP)], axis=0)
    s = lax.dot_general(qm, k.astype(BF16), (((1,), (1,)), ((), ())),
                        preferred_element_type=F32) * (HEAD_DIM ** -0.5)
    mask4 = jnp.concatenate([mask] * HEADS_PER_GROUP, axis=0)
    s = jnp.where(mask4, s, NEG_INF)
    m = jnp.max(s, axis=-1, keepdims=True)
    p = jnp.exp(s - m)
    l = jnp.sum(p, axis=-1, keepdims=True)
    pv = jnp.dot(p.astype(BF16), v.astype(BF16), preferred_element_type=F32)
    o_all = pv / l
    lse = m + jnp.log(l)
    o = jnp.zeros_like(q)
    lse_b = jnp.zeros_like(q)
    for h in range(HEADS_PER_GROUP):
        sel = lane_head == h
        o = jnp.where(sel, o_all[h * BLOCK:(h + 1) * BLOCK], o)
        lse_b = jnp.where(sel, lse[h * BLOCK:(h + 1) * BLOCK], lse_b)
    return o, lse_b


def _attn_kernel(qkv_ref, o_ref, qs, ks, vs, o_scr, l_scr):
    S = qkv_ref.shape[1]
    n_chunks = ATTN_OUT_W // LANE
    lane_head = lax.broadcasted_iota(jnp.int32, (BLOCK, ATTN_OUT_W), 1) // HEAD_DIM
    qi = lax.broadcasted_iota(jnp.int32, (BLOCK, 2 * BLOCK), 0)
    ki = lax.broadcasted_iota(jnp.int32, (BLOCK, 2 * BLOCK), 1)
    dist = qi + BLOCK - ki

    for gi, (window, d) in enumerate(ATTN_GROUPS):
        span = window // d
        L = S // d
        nb = L // BLOCK
        band = (dist >= 0) & (dist <= span)
        cols = [part * N_ATTN_HEADS * HEAD_DIM + gi * ATTN_OUT_W for part in range(3)]
        if d > 1:
            for scr, col in zip((qs, ks, vs), cols):
                for c in range(n_chunks):
                    scr[c] = qkv_ref[0, :, col + c * LANE:col + (c + 1) * LANE].astype(F32)

        def load(scr, col, start, d=d):
            if d == 1:
                return qkv_ref[0, pl.ds(start, BLOCK), col:col + ATTN_OUT_W].astype(F32)
            return jnp.concatenate(
                [scr[c, pl.ds(start, BLOCK, stride=d), :] for c in range(n_chunks)], axis=1)

        def body(idx, carry, gi=gi, d=d, nb=nb, band=band, cols=cols, load=load):
            r = idx // nb
            n = idx % nb
            start = r + d * BLOCK * n
            if d == 1:
                start = pl.multiple_of(start, BLOCK)
            q = load(qs, cols[0], start)
            k = load(ks, cols[1], start)
            v = load(vs, cols[2], start)
            if nb > 1:
                pstart = r + d * BLOCK * jnp.maximum(n - 1, 0)
                if d == 1:
                    pstart = pl.multiple_of(pstart, BLOCK)
                k = jnp.concatenate([load(ks, cols[1], pstart), k], axis=0)
                v = jnp.concatenate([load(vs, cols[2], pstart), v], axis=0)
                mask = band & ((ki + BLOCK * jnp.minimum(n, 1)) >= BLOCK)
            else:
                mask = band[:, BLOCK:]
            o, lse_b = _attn_block(q, k, v, mask, lane_head)
            for c in range(n_chunks):
                if d == 1:
                    o_scr[gi, c, pl.ds(start, BLOCK), :] = o[:, c * LANE:(c + 1) * LANE]
                    l_scr[gi, c, pl.ds(start, BLOCK), :] = lse_b[:, c * LANE:(c + 1) * LANE]
                else:
                    o_scr[gi, c, pl.ds(start, BLOCK, stride=d), :] = o[:, c * LANE:(c + 1) * LANE]
                    l_scr[gi, c, pl.ds(start, BLOCK, stride=d), :] = lse_b[:, c * LANE:(c + 1) * LANE]
            return carry

        lax.fori_loop(0, d * nb, body, 0)

    n_groups = len(ATTN_GROUPS)
    rows = 2 * BLOCK

    def merge(i, carry):
        r0 = pl.multiple_of(i * rows, rows)
        for c in range(n_chunks):
            ls = [l_scr[g, c, pl.ds(r0, rows), :] for g in range(n_groups)]
            m = functools.reduce(jnp.maximum, ls)
            es = [jnp.exp(l - m) for l in ls]
            den = functools.reduce(lambda a, b: a + b, es)
            acc = None
            for g in range(n_groups):
                term = (es[g] / den) * o_scr[g, c, pl.ds(r0, rows), :]
                acc = term if acc is None else acc + term
            o_ref[0, pl.ds(r0, rows), c * LANE:(c + 1) * LANE] = acc.astype(o_ref.dtype)
        return carry

    lax.fori_loop(0, S // rows, merge, 0)


def _attn(qkv):
    B, S, _ = qkv.shape
    n_chunks = ATTN_OUT_W // LANE
    n_groups = len(ATTN_GROUPS)
    for window, d in ATTN_GROUPS:
        assert window // d == BLOCK and S % (d * BLOCK) == 0
    return pl.pallas_call(
        _attn_kernel,
        grid=(B,),
        in_specs=[pl.BlockSpec((1, S, QKV_W), lambda b: (b, 0, 0))],
        out_specs=pl.BlockSpec((1, S, ATTN_OUT_W), lambda b: (b, 0, 0)),
        out_shape=jax.ShapeDtypeStruct((B, S, ATTN_OUT_W), BF16),
        scratch_shapes=[
            pltpu.VMEM((n_chunks, S, LANE), F32),
            pltpu.VMEM((n_chunks, S, LANE), F32),
            pltpu.VMEM((n_chunks, S, LANE), F32),
            pltpu.VMEM((n_groups, n_chunks, S, LANE), F32),
            pltpu.VMEM((n_groups, n_chunks, S, LANE), F32),
        ],
        compiler_params=pltpu.CompilerParams(
            dimension_semantics=("arbitrary",), vmem_limit_bytes=VMEM_LIMIT),
        name="attn",
    )(qkv)


def _ssm_kernel(u_ref, wbu_ref, wc_ref, a_ref, d_ref, y_ref, bu_scr, st_scr):
    nb, tc, _ = u_ref.shape
    rows = nb * tc
    n_chunks = SSM_LANES // LANE
    tile_n = 2 * LANE

    @pl.when(pl.program_id(0) == 0)
    def _():
        st_scr[...] = jnp.zeros_like(st_scr)

    u = u_ref[...].reshape(rows, SSM_W)
    ub = u.astype(BF16)
    for i in range(SSM_LANES // tile_n):
        k0 = (i * tile_n // SSM_STATE * SSM_CH) // LANE * LANE
        usl = ub[:, k0:k0 + LANE]
        for ri in range(2):
            bu = jnp.dot(usl, wbu_ref[ri, i], preferred_element_type=F32)
            for c in range(tile_n // LANE):
                bu_scr[ri, i * (tile_n // LANE) + c] = bu[:, c * LANE:(c + 1) * LANE]

    for cp in range(n_chunks // SSM_PAIR):
        chunks = [cp * SSM_PAIR + j for j in range(SSM_PAIR)]
        a_re = [a_ref[0, c] for c in chunks]
        a_im = [a_ref[1, c] for c in chunks]
        init = tuple(st_scr[ri, c] for c in chunks for ri in range(2))

        def step(t, carry, chunks=chunks, a_re=a_re, a_im=a_im):
            out = []
            for j, c in enumerate(chunks):
                xr, xi = carry[2 * j], carry[2 * j + 1]
                br = bu_scr[0, c, pl.ds(t, nb, stride=tc), :]
                bi = bu_scr[1, c, pl.ds(t, nb, stride=tc), :]
                nr = a_re[j] * xr - a_im[j] * xi + br
                ni = a_re[j] * xi + a_im[j] * xr + bi
                bu_scr[0, c, pl.ds(t, nb, stride=tc), :] = nr
                bu_scr[1, c, pl.ds(t, nb, stride=tc), :] = ni
                out += [nr, ni]
            return tuple(out)

        fin = lax.fori_loop(0, tc, step, init, unroll=4)
        for j, c in enumerate(chunks):
            st_scr[0, c] = fin[2 * j]
            st_scr[1, c] = fin[2 * j + 1]

    out_tile = 2 * LANE
    k_chunks = (out_tile // SSM_CH * SSM_STATE) // LANE
    ys = []
    for j in range(SSM_W // out_tile):
        acc = None
        for ri in range(2):
            xs = jnp.concatenate(
                [bu_scr[ri, j * k_chunks + c] for c in range(k_chunks)], axis=1).astype(BF16)
            part = jnp.dot(xs, wc_ref[ri, j], preferred_element_type=F32)
            acc = part if acc is None else acc + part
        ys.append(acc)
    y = jnp.concatenate(ys, axis=1) + d_ref[...] * u
    y_ref[...] = jax.nn.gelu(y).astype(y_ref.dtype).reshape(nb, tc, SSM_W)


def _ssm_params(a_re, a_im, log_dt, b_re, b_im, c_re, c_im, d_skip, nb):
    lr, li = a_re.astype(F32), a_im.astype(F32)
    dt = jnp.exp(log_dt.astype(F32))[:, None]
    mag = jnp.exp(lr * dt)
    ab_re, ab_im = mag * jnp.cos(li * dt), mag * jnp.sin(li * dt)
    den = lr * lr + li * li
    nr, ni = ab_re - 1.0, ab_im
    f_re = (nr * lr + ni * li) / den
    f_im = (ni * lr - nr * li) / den
    br, bi = b_re.astype(F32), b_im.astype(F32)
    bb_re = f_re[..., None] * br - f_im[..., None] * bi
    bb_im = f_re[..., None] * bi + f_im[..., None] * br
    eye = jnp.eye(SSM_GROUPS, dtype=F32)
    tile_n = 2 * LANE

    def bu_tiles(bb):
        full = jnp.einsum('gnc,gh->gchn', bb, eye).reshape(SSM_W, SSM_LANES)
        tiles = []
        for i in range(SSM_LANES // tile_n):
            k0 = (i * tile_n // SSM_STATE * SSM_CH) // LANE * LANE
            tiles.append(full[k0:k0 + LANE, i * tile_n:(i + 1) * tile_n])
        return jnp.stack(tiles)

    def c_tiles(c):
        full = jnp.einsum('gcn,gh->gnhc', c, eye).reshape(SSM_LANES, SSM_W)
        k_rows = tile_n // SSM_CH * SSM_STATE
        return jnp.stack([full[j * k_rows:(j + 1) * k_rows, j * tile_n:(j + 1) * tile_n]
                          for j in range(SSM_W // tile_n)])

    wbu = jnp.stack([bu_tiles(bb_re), bu_tiles(bb_im)]).astype(BF16)
    wc = jnp.stack([c_tiles(c_re.astype(F32)), c_tiles(-c_im.astype(F32))]).astype(BF16)
    n_chunks = SSM_LANES // LANE
    a = jnp.stack([ab_re.reshape(n_chunks, 1, LANE), ab_im.reshape(n_chunks, 1, LANE)])
    a = jnp.broadcast_to(a, (2, n_chunks, nb, LANE))
    d = d_skip.astype(F32).reshape(1, SSM_W)
    return wbu, wc, a, d


def _ssm(u, wbu, wc, a, d):
    B, S, _ = u.shape
    tc = SSM_TC
    n_chunks = SSM_LANES // LANE
    return pl.pallas_call(
        _ssm_kernel,
        grid=(S // tc,),
        in_specs=[
            pl.BlockSpec((B, tc, SSM_W), lambda i: (0, i, 0)),
            _const_spec(wbu.shape),
            _const_spec(wc.shape),
            _const_spec(a.shape),
            _const_spec(d.shape),
        ],
        out_specs=pl.BlockSpec((B, tc, SSM_W), lambda i: (0, i, 0)),
        out_shape=jax.ShapeDtypeStruct((B, S, SSM_W), BF16),
        scratch_shapes=[
            pltpu.VMEM((2, n_chunks, B * tc, LANE), F32),
            pltpu.VMEM((2, n_chunks, B, LANE), F32),
        ],
        compiler_params=pltpu.CompilerParams(
            dimension_semantics=("arbitrary",), vmem_limit_bytes=VMEM_LIMIT),
        name="ssm",
    )(u, wbu, wc, a, d)


def _mix_kernel(x_ref, attn_ref, y_ref, g_ref, wg_ref, wao_ref, wglu_ref, wout_ref, o_ref):
    x = x_ref[0]
    h = _rmsnorm(x, g_ref[...]).astype(BF16)
    gate = jax.nn.sigmoid(jnp.dot(h, wg_ref[...], preferred_element_type=F32))
    attn_d = jnp.dot(attn_ref[0], wao_ref[...], preferred_element_type=F32)
    z = jnp.dot(y_ref[0], wglu_ref[...], preferred_element_type=F32)
    ssm_out = z[:, :D_MODEL] * jax.nn.sigmoid(z[:, D_MODEL:])
    merged = gate[:, :D_MODEL] * attn_d + gate[:, D_MODEL:] * ssm_out
    o_ref[0] = x + jnp.dot(merged.astype(BF16), wout_ref[...], preferred_element_type=F32)


def _mix(x, attn, y, g, wg, wao, wglu, wout):
    B, S, D = x.shape
    tm = TM_MIX
    tok = lambda w: pl.BlockSpec((1, tm, w), lambda b, i: (b, i, 0))
    return pl.pallas_call(
        _mix_kernel,
        grid=(B, S // tm),
        in_specs=[tok(D), tok(ATTN_OUT_W), tok(SSM_W), _const_spec(g.shape), _const_spec(wg.shape),
                  _const_spec(wao.shape), _const_spec(wglu.shape), _const_spec(wout.shape)],
        out_specs=tok(D),
        out_shape=jax.ShapeDtypeStruct((B, S, D), F32),
        compiler_params=pltpu.CompilerParams(
            dimension_semantics=("arbitrary", "arbitrary"), vmem_limit_bytes=VMEM_LIMIT),
        name="mix",
    )(x, attn, y, g, wg, wao, wglu, wout)


def _ffn_kernel(x_ref, g2_ref, gf_ref, wgate_ref, wup_ref, wdown_ref, o_ref):
    x = x_ref[0]
    h = _rmsnorm(x, g2_ref[...]).astype(BF16)
    gate = jnp.dot(h, wgate_ref[...], preferred_element_type=F32)
    up = jnp.dot(h, wup_ref[...], preferred_element_type=F32)
    act = (jax.nn.silu(gate) * up).astype(BF16)
    x = x + jnp.dot(act, wdown_ref[...], preferred_element_type=F32)
    o_ref[0] = _rmsnorm(x, gf_ref[...])


def _ffn(x, g2, gf, wgate, wup, wdown):
    B, S, D = x.shape
    tm = TM_FFN
    tok = pl.BlockSpec((1, tm, D), lambda b, i: (b, i, 0))
    return pl.pallas_call(
        _ffn_kernel,
        grid=(B, S // tm),
        in_specs=[tok, _const_spec(g2.shape), _const_spec(gf.shape), _const_spec(wgate.shape),
                  _const_spec(wup.shape), _const_spec(wdown.shape)],
        out_specs=tok,
        out_shape=jax.ShapeDtypeStruct((B, S, D), F32),
        compiler_params=pltpu.CompilerParams(
            dimension_semantics=("arbitrary", "arbitrary"), vmem_limit_bytes=VMEM_LIMIT),
        name="ffn",
    )(x, g2, gf, wgate, wup, wdown)


def kernel(x, norm_mix_g, w_in, ssm_a_re, ssm_a_im, ssm_log_dt, ssm_b_re, ssm_b_im, ssm_c_re, ssm_c_im, ssm_d, w_glu, w_attn_out, w_out, norm_ffn_g, w_ffn_gate, w_ffn_up, w_ffn_down, norm_final_g):
    B, S, D = x.shape
    assert D == D_MODEL and norm_mix_g.shape[0] == 1, "single-layer block expected"
    layer = 0
    g_mix = norm_mix_g[layer].reshape(1, D).astype(F32)
    w_in_l = w_in[layer]
    w_qkvu = w_in_l[:, :QKVU_W].astype(BF16)
    w_gate = w_in_l[:, QKVU_W:].astype(BF16)

    qkv, u = _inproj(x, g_mix, w_qkvu, _rope_tables(S))
    attn = _attn(qkv)
    wbu, wc, a, d = _ssm_params(
        ssm_a_re[layer], ssm_a_im[layer], ssm_log_dt[layer], ssm_b_re[layer], ssm_b_im[layer],
        ssm_c_re[layer], ssm_c_im[layer], ssm_d[layer], B)
    y = _ssm(u, wbu, wc, a, d)
    x1 = _mix(x, attn, y, g_mix, w_gate, w_attn_out[layer].astype(BF16),
              w_glu[layer].astype(BF16), w_out[layer].astype(BF16))
    return _ffn(x1, norm_ffn_g[layer].reshape(1, D).astype(F32),
                norm_final_g.reshape(1, D).astype(F32), w_ffn_gate[layer].astype(BF16),
                w_ffn_up[layer].astype(BF16), w_ffn_down[layer].astype(BF16))
```

```python
import functools

import jax
import jax.numpy as jnp
from jax import lax
from jax.experimental import pallas as pl
from jax.experimental.pallas import tpu as pltpu

F32 = jnp.float32
BF16 = jnp.bfloat16

D_MODEL = 1024
HEAD_DIM = 64
HEADS_PER_GROUP = 4
ATTN_GROUPS = ((128, 1), (512, 4), (2048, 16))
N_ATTN_HEADS = HEADS_PER_GROUP * len(ATTN_GROUPS)
ATTN_OUT_W = HEADS_PER_GROUP * HEAD_DIM
ROPE_DIM = HEAD_DIM // 4
ROPE_THETA = 500000.0
BLOCK = 128
SSM_CH = 16
SSM_GROUPS = 32
SSM_W = SSM_CH * SSM_GROUPS
SSM_STATE = 64
SSM_LANES = SSM_GROUPS * SSM_STATE
D_FF = 2816
QK_W = 2 * N_ATTN_HEADS * HEAD_DIM
QKV_W = 3 * N_ATTN_HEADS * HEAD_DIM
QKVU_W = QKV_W + SSM_W
RMS_EPS = 1e-6
NEG_INF = -1e30

LANE = 128
VMEM_LIMIT = 56 * 1024 * 1024

TS_INPROJ = 512
TM_MIX = 512
TM_FFN = 512
SSM_TC = 32
ATTN_UNROLL = 2
SSM_PAIR = 2
SSM_UNROLL = 8
SUBLANE = 8


def _rmsnorm(x, g):
    return x * lax.rsqrt(jnp.mean(x * x, axis=-1, keepdims=True) + RMS_EPS) * g


def _const_spec(shape):
    return pl.BlockSpec(shape, lambda *_: (0,) * len(shape), pipeline_mode=pl.Buffered(1))


def _inproj_kernel(x_ref, g_ref, w_ref, rc_ref, ra_ref, rb_ref, qkv_ref, u_ref):
    h = _rmsnorm(x_ref[0], g_ref[...]).astype(BF16)
    p = jnp.dot(h, w_ref[...], preferred_element_type=F32)
    rc, ra, rb = rc_ref[...], ra_ref[...], rb_ref[...]
    for c in range(QK_W // LANE):
        t = p[:, c * LANE:(c + 1) * LANE]
        half = ROPE_DIM // 2
        t = t * rc + pltpu.roll(t, LANE - half, 1) * ra + pltpu.roll(t, half, 1) * rb
        qkv_ref[0, :, c * LANE:(c + 1) * LANE] = t.astype(BF16)
    qkv_ref[0, :, QK_W:QKV_W] = p[:, QK_W:QKV_W].astype(BF16)
    u_ref[0] = p[:, QKV_W:QKVU_W]


def _rope_tables(S):
    half = ROPE_DIM // 2
    pos = jnp.arange(S, dtype=F32)
    inv = jnp.power(jnp.float32(ROPE_THETA), -jnp.arange(half, dtype=F32) * 2.0 / ROPE_DIM)
    ang = pos[:, None] * inv[None, :]
    cos, sin = jnp.cos(ang), jnp.sin(ang)
    ones = jnp.ones((S, HEAD_DIM - ROPE_DIM), F32)
    zeros_h = jnp.zeros((S, half), F32)
    zeros_r = jnp.zeros((S, HEAD_DIM - ROPE_DIM), F32)
    per_head_c = jnp.concatenate([cos, cos, ones], axis=1)
    per_head_a = jnp.concatenate([-sin, zeros_h, zeros_r], axis=1)
    per_head_b = jnp.concatenate([zeros_h, sin, zeros_r], axis=1)
    rep = LANE // HEAD_DIM
    return tuple(jnp.tile(t, (1, rep)) for t in (per_head_c, per_head_a, per_head_b))


def _inproj(x, g, w, rope):
    B, S, D = x.shape
    ts = TS_INPROJ
    tab_spec = pl.BlockSpec((ts, LANE), lambda b, i: (i, 0))
    return pl.pallas_call(
        _inproj_kernel,
        grid=(B, S // ts),
        in_specs=[
            pl.BlockSpec((1, ts, D), lambda b, i: (b, i, 0)),
            _const_spec((1, D)),
            _const_spec((D, QKVU_W)),
            tab_spec, tab_spec, tab_spec,
        ],
        out_specs=[
            pl.BlockSpec((1, ts, QKV_W), lambda b, i: (b, i, 0)),
            pl.BlockSpec((1, ts, SSM_W), lambda b, i: (b, i, 0)),
        ],
        out_shape=[
            jax.ShapeDtypeStruct((B, S, QKV_W), BF16),
            jax.ShapeDtypeStruct((B, S, SSM_W), F32),
        ],
        compiler_params=pltpu.CompilerParams(
            dimension_semantics=("arbitrary", "arbitrary"), vmem_limit_bytes=VMEM_LIMIT),
        name="inproj",
    )(x, g, w, *rope)


def _attn_block(q, k, v, mask, lane_head):
    zero = jnp.zeros_like(q)
    qm = jnp.concatenate(
        [jnp.where(lane_head == h, q, zero).astype(BF16) for h in range(HEADS_PER_GROUP)], axis=0)
    s = lax.dot_general(qm, k.astype(BF16), (((1,), (1,)), ((), ())),
                        preferred_element_type=F32) * (HEAD_DIM ** -0.5)
    mask4 = jnp.concatenate([mask] * HEADS_PER_GROUP, axis=0)
    s = jnp.where(mask4, s, NEG_INF)
    m = jnp.max(s, axis=-1, keepdims=True)
    p = jnp.exp(s - m)
    l = jnp.sum(p, axis=-1, keepdims=True)
    pv = jnp.dot(p.astype(BF16), v.astype(BF16), preferred_element_type=F32)
    o_all = pv / l
    lse = m + jnp.log(l)
    o = jnp.zeros_like(q)
    lse_b = jnp.zeros_like(q)
    for h in range(HEADS_PER_GROUP):
        sel = lane_head == h
        o = jnp.where(sel, o_all[h * BLOCK:(h + 1) * BLOCK], o)
        lse_b = jnp.where(sel, lse[h * BLOCK:(h + 1) * BLOCK], lse_b)
    return o, lse_b


def _attn_kernel(qkv_ref, o_ref, qs, ks, vs, o_scr, l_scr):
    S = qkv_ref.shape[1]
    n_chunks = ATTN_OUT_W // LANE
    lane_head = lax.broadcasted_iota(jnp.int32, (BLOCK, ATTN_OUT_W), 1) // HEAD_DIM
    qi = lax.broadcasted_iota(jnp.int32, (BLOCK, 2 * BLOCK), 0)
    ki = lax.broadcasted_iota(jnp.int32, (BLOCK, 2 * BLOCK), 1)
    dist = qi + BLOCK - ki

    for gi, (window, d) in enumerate(ATTN_GROUPS):
        span = window // d
        L = S // d
        nb = L // BLOCK
        band = (dist >= 0) & (dist <= span)
        cols = [part * N_ATTN_HEADS * HEAD_DIM + gi * ATTN_OUT_W for part in range(3)]
        if d > 1:
            for scr, col in zip((qs, ks, vs), cols):
                for c in range(n_chunks):
                    scr[c] = qkv_ref[0, :, col + c * LANE:col + (c + 1) * LANE].astype(F32)

        def load(scr, col, start, d=d):
            if d == 1:
                return qkv_ref[0, pl.ds(start, BLOCK), col:col + ATTN_OUT_W].astype(F32)
            return jnp.concatenate(
                [scr[c, pl.ds(start, BLOCK, stride=d), :] for c in range(n_chunks)], axis=1)

        def body(idx, carry, gi=gi, d=d, nb=nb, band=band, cols=cols, load=load):
            r = idx // nb
            n = idx % nb
            start = r + d * BLOCK * n
            if d == 1:
                start = pl.multiple_of(start, BLOCK)
            q = load(qs, cols[0], start)
            k = load(ks, cols[1], start)
            v = load(vs, cols[2], start)
            if nb > 1:
                pstart = r + d * BLOCK * jnp.maximum(n - 1, 0)
                if d == 1:
                    pstart = pl.multiple_of(pstart, BLOCK)
                k = jnp.concatenate([load(ks, cols[1], pstart), k], axis=0)
                v = jnp.concatenate([load(vs, cols[2], pstart), v], axis=0)
                mask = band & ((ki + BLOCK * jnp.minimum(n, 1)) >= BLOCK)
            else:
                mask = band[:, BLOCK:]
            o, lse_b = _attn_block(q, k, v, mask, lane_head)
            for c in range(n_chunks):
                if d == 1:
                    o_scr[gi, c, pl.ds(start, BLOCK), :] = o[:, c * LANE:(c + 1) * LANE]
                    l_scr[gi, c, pl.ds(start, BLOCK), :] = lse_b[:, c * LANE:(c + 1) * LANE]
                else:
                    o_scr[gi, c, pl.ds(start, BLOCK, stride=d), :] = o[:, c * LANE:(c + 1) * LANE]
                    l_scr[gi, c, pl.ds(start, BLOCK, stride=d), :] = lse_b[:, c * LANE:(c + 1) * LANE]
            return carry

        lax.fori_loop(0, d * nb, body, 0, unroll=ATTN_UNROLL)

    n_groups = len(ATTN_GROUPS)
    rows = 2 * BLOCK

    def merge(i, carry):
        r0 = pl.multiple_of(i * rows, rows)
        for c in range(n_chunks):
            ls = [l_scr[g, c, pl.ds(r0, rows), :] for g in range(n_groups)]
            m = functools.reduce(jnp.maximum, ls)
            es = [jnp.exp(l - m) for l in ls]
            den = functools.reduce(lambda a, b: a + b, es)
            acc = None
            for g in range(n_groups):
                term = (es[g] / den) * o_scr[g, c, pl.ds(r0, rows), :]
                acc = term if acc is None else acc + term
            o_ref[0, pl.ds(r0, rows), c * LANE:(c + 1) * LANE] = acc.astype(o_ref.dtype)
        return carry

    lax.fori_loop(0, S // rows, merge, 0)


def _attn(qkv):
    B, S, _ = qkv.shape
    n_chunks = ATTN_OUT_W // LANE
    n_groups = len(ATTN_GROUPS)
    for window, d in ATTN_GROUPS:
        assert window // d == BLOCK and S % (d * BLOCK) == 0
    return pl.pallas_call(
        _attn_kernel,
        grid=(B,),
        in_specs=[pl.BlockSpec((1, S, QKV_W), lambda b: (b, 0, 0))],
        out_specs=pl.BlockSpec((1, S, ATTN_OUT_W), lambda b: (b, 0, 0)),
        out_shape=jax.ShapeDtypeStruct((B, S, ATTN_OUT_W), BF16),
        scratch_shapes=[
            pltpu.VMEM((n_chunks, S, LANE), F32),
            pltpu.VMEM((n_chunks, S, LANE), F32),
            pltpu.VMEM((n_chunks, S, LANE), F32),
            pltpu.VMEM((n_groups, n_chunks, S, LANE), F32),
            pltpu.VMEM((n_groups, n_chunks, S, LANE), F32),
        ],
        compiler_params=pltpu.CompilerParams(
            dimension_semantics=("arbitrary",), vmem_limit_bytes=VMEM_LIMIT),
        name="attn",
    )(qkv)


def _ssm_kernel(u_ref, p_ref, pt_ref, wbu_ref, wc_ref, a_ref, d_ref, y_ref, bu_scr, st_scr):
    nb, tc, _ = u_ref.shape
    rows = nb * tc
    n_half = nb // SUBLANE
    hrows = SUBLANE * tc
    n_chunks = SSM_LANES // LANE
    tile_n = 2 * LANE

    @pl.when(pl.program_id(0) == 0)
    def _():
        st_scr[...] = jnp.zeros_like(st_scr)

    def permute(pm, x):
        return jnp.concatenate(
            [jnp.dot(pm, x[h * hrows:(h + 1) * hrows], preferred_element_type=F32)
             for h in range(n_half)], axis=0)

    u = u_ref[...].reshape(rows, SSM_W)
    u_hi = u.astype(BF16)
    r1 = u - u_hi.astype(F32)
    u_mid = r1.astype(BF16)
    u_lo = (r1 - u_mid.astype(F32)).astype(BF16)
    p_fwd = p_ref[...]
    ut_hi = permute(p_fwd, u_hi)
    u = ut_hi + permute(p_fwd, u_mid) + permute(p_fwd, u_lo)
    ub = ut_hi.astype(BF16)
    for i in range(SSM_LANES // tile_n):
        k0 = (i * tile_n // SSM_STATE * SSM_CH) // LANE * LANE
        usl = ub[:, k0:k0 + LANE]
        for ri in range(2):
            bu = jnp.dot(usl, wbu_ref[ri, i], preferred_element_type=F32)
            for c in range(tile_n // LANE):
                bu_scr[ri, i * (tile_n // LANE) + c] = bu[:, c * LANE:(c + 1) * LANE]

    for cp in range(n_chunks // SSM_PAIR):
        chains = [(cp * SSM_PAIR + j, h) for j in range(SSM_PAIR) for h in range(n_half)]
        a_re = {c: a_ref[0, c] for c, _ in chains}
        a_im = {c: a_ref[1, c] for c, _ in chains}
        init = tuple(st_scr[ri, c, h] for c, h in chains for ri in range(2))

        def step(t, carry, chains=chains, a_re=a_re, a_im=a_im):
            out = []
            for j, (c, h) in enumerate(chains):
                xr, xi = carry[2 * j], carry[2 * j + 1]
                row = pl.ds(pl.multiple_of(h * hrows + SUBLANE * t, SUBLANE), SUBLANE)
                nr = a_re[c] * xr - a_im[c] * xi + bu_scr[0, c, row, :]
                ni = a_re[c] * xi + a_im[c] * xr + bu_scr[1, c, row, :]
                bu_scr[0, c, row, :] = nr
                bu_scr[1, c, row, :] = ni
                out += [nr, ni]
            return tuple(out)

        fin = lax.fori_loop(0, tc, step, init, unroll=SSM_UNROLL)
        for j, (c, h) in enumerate(chains):
            st_scr[0, c, h] = fin[2 * j]
            st_scr[1, c, h] = fin[2 * j + 1]

    out_tile = 2 * LANE
    k_chunks = (out_tile // SSM_CH * SSM_STATE) // LANE
    ys = []
    for j in range(SSM_W // out_tile):
        acc = None
        for ri in range(2):
            xs = jnp.concatenate(
                [bu_scr[ri, j * k_chunks + c] for c in range(k_chunks)], axis=1).astype(BF16)
            part = jnp.dot(xs, wc_ref[ri, j], preferred_element_type=F32)
            acc = part if acc is None else acc + part
        ys.append(acc)
    y = jnp.concatenate(ys, axis=1) + d_ref[...] * u
    y = jax.nn.gelu(y).astype(BF16)
    y = permute(pt_ref[...], y).astype(y_ref.dtype)
    y_ref[...] = y.reshape(nb, tc, SSM_W)


def _ssm_params(a_re, a_im, log_dt, b_re, b_im, c_re, c_im, d_skip):
    lr, li = a_re.astype(F32), a_im.astype(F32)
    dt = jnp.exp(log_dt.astype(F32))[:, None]
    mag = jnp.exp(lr * dt)
    ab_re, ab_im = mag * jnp.cos(li * dt), mag * jnp.sin(li * dt)
    den = lr * lr + li * li
    nr, ni = ab_re - 1.0, ab_im
    f_re = (nr * lr + ni * li) / den
    f_im = (ni * lr - nr * li) / den
    br, bi = b_re.astype(F32), b_im.astype(F32)
    bb_re = f_re[..., None] * br - f_im[..., None] * bi
    bb_im = f_re[..., None] * bi + f_im[..., None] * br
    eye = jnp.eye(SSM_GROUPS, dtype=F32)
    tile_n = 2 * LANE

    def bu_tiles(bb):
        full = jnp.einsum('gnc,gh->gchn', bb, eye).reshape(SSM_W, SSM_LANES)
        tiles = []
        for i in range(SSM_LANES // tile_n):
            k0 = (i * tile_n // SSM_STATE * SSM_CH) // LANE * LANE
            tiles.append(full[k0:k0 + LANE, i * tile_n:(i + 1) * tile_n])
        return jnp.stack(tiles)

    def c_tiles(c):
        full = jnp.einsum('gcn,gh->gnhc', c, eye).reshape(SSM_LANES, SSM_W)
        k_rows = tile_n // SSM_CH * SSM_STATE
        return jnp.stack([full[j * k_rows:(j + 1) * k_rows, j * tile_n:(j + 1) * tile_n]
                          for j in range(SSM_W // tile_n)])

    wbu = jnp.stack([bu_tiles(bb_re), bu_tiles(bb_im)]).astype(BF16)
    wc = jnp.stack([c_tiles(c_re.astype(F32)), c_tiles(-c_im.astype(F32))]).astype(BF16)
    n_chunks = SSM_LANES // LANE
    a = jnp.stack([ab_re.reshape(n_chunks, 1, LANE), ab_im.reshape(n_chunks, 1, LANE)])
    a = jnp.broadcast_to(a, (2, n_chunks, SUBLANE, LANE))
    d = d_skip.astype(F32).reshape(1, SSM_W)
    return wbu, wc, a, d


def _ssm_row_permutation(tc):
    dst = jnp.arange(SUBLANE * tc)
    src = (dst % SUBLANE) * tc + dst // SUBLANE
    p = (src[:, None] == jnp.arange(SUBLANE * tc)[None, :]).astype(BF16)
    return p, p.T


def _ssm(u, wbu, wc, a, d):
    B, S, _ = u.shape
    tc = SSM_TC
    n_chunks = SSM_LANES // LANE
    assert B % SUBLANE == 0 and S % tc == 0
    p_fwd, p_bwd = _ssm_row_permutation(tc)
    return pl.pallas_call(
        _ssm_kernel,
        grid=(S // tc,),
        in_specs=[
            pl.BlockSpec((B, tc, SSM_W), lambda i: (0, i, 0)),
            _const_spec(p_fwd.shape),
            _const_spec(p_bwd.shape),
            _const_spec(wbu.shape),
            _const_spec(wc.shape),
            _const_spec(a.shape),
            _const_spec(d.shape),
        ],
        out_specs=pl.BlockSpec((B, tc, SSM_W), lambda i: (0, i, 0)),
        out_shape=jax.ShapeDtypeStruct((B, S, SSM_W), BF16),
        scratch_shapes=[
            pltpu.VMEM((2, n_chunks, B * tc, LANE), F32),
            pltpu.VMEM((2, n_chunks, B // SUBLANE, SUBLANE, LANE), F32),
        ],
        compiler_params=pltpu.CompilerParams(
            dimension_semantics=("arbitrary",), vmem_limit_bytes=VMEM_LIMIT),
        name="ssm",
    )(u, p_fwd, p_bwd, wbu, wc, a, d)


def _mix_kernel(x_ref, attn_ref, y_ref, g_ref, wg_ref, wao_ref, wglu_ref, wout_ref, o_ref):
    x = x_ref[0]
    h = _rmsnorm(x, g_ref[...]).astype(BF16)
    gate = jax.nn.sigmoid(jnp.dot(h, wg_ref[...], preferred_element_type=F32))
    attn_d = jnp.dot(attn_ref[0], wao_ref[...], preferred_element_type=F32)
    z = jnp.dot(y_ref[0], wglu_ref[...], preferred_element_type=F32)
    ssm_out = z[:, :D_MODEL] * jax.nn.sigmoid(z[:, D_MODEL:])
    merged = gate[:, :D_MODEL] * attn_d + gate[:, D_MODEL:] * ssm_out
    o_ref[0] = x + jnp.dot(merged.astype(BF16), wout_ref[...], preferred_element_type=F32)


def _mix(x, attn, y, g, wg, wao, wglu, wout):
    B, S, D = x.shape
    tm = TM_MIX
    tok = lambda w: pl.BlockSpec((1, tm, w), lambda b, i: (b, i, 0))
    return pl.pallas_call(
        _mix_kernel,
        grid=(B, S // tm),
        in_specs=[tok(D), tok(ATTN_OUT_W), tok(SSM_W), _const_spec(g.shape), _const_spec(wg.shape),
                  _const_spec(wao.shape), _const_spec(wglu.shape), _const_spec(wout.shape)],
        out_specs=tok(D),
        out_shape=jax.ShapeDtypeStruct((B, S, D), F32),
        compiler_params=pltpu.CompilerParams(
            dimension_semantics=("arbitrary", "arbitrary"), vmem_limit_bytes=VMEM_LIMIT),
        name="mix",
    )(x, attn, y, g, wg, wao, wglu, wout)


def _ffn_kernel(x_ref, g2_ref, gf_ref, wgate_ref, wup_ref, wdown_ref, o_ref):
    x = x_ref[0]
    h = _rmsnorm(x, g2_ref[...]).astype(BF16)
    gate = jnp.dot(h, wgate_ref[...], preferred_element_type=F32)
    up = jnp.dot(h, wup_ref[...], preferred_element_type=F32)
    act = (jax.nn.silu(gate) * up).astype(BF16)
    x = x + jnp.dot(act, wdown_ref[...], preferred_element_type=F32)
    o_ref[0] = _rmsnorm(x, gf_ref[...])


def _ffn(x, g2, gf, wgate, wup, wdown):
    B, S, D = x.shape
    tm = TM_FFN
    tok = pl.BlockSpec((1, tm, D), lambda b, i: (b, i, 0))
    return pl.pallas_call(
        _ffn_kernel,
        grid=(B, S // tm),
        in_specs=[tok, _const_spec(g2.shape), _const_spec(gf.shape), _const_spec(wgate.shape),
                  _const_spec(wup.shape), _const_spec(wdown.shape)],
        out_specs=tok,
        out_shape=jax.ShapeDtypeStruct((B, S, D), F32),
        compiler_params=pltpu.CompilerParams(
            dimension_semantics=("arbitrary", "arbitrary"), vmem_limit_bytes=VMEM_LIMIT),
        name="ffn",
    )(x, g2, gf, wgate, wup, wdown)


def kernel(x, norm_mix_g, w_in, ssm_a_re, ssm_a_im, ssm_log_dt, ssm_b_re, ssm_b_im, ssm_c_re, ssm_c_im, ssm_d, w_glu, w_attn_out, w_out, norm_ffn_g, w_ffn_gate, w_ffn_up, w_ffn_down, norm_final_g):
    B, S, D = x.shape
    assert D == D_MODEL and norm_mix_g.shape[0] == 1, "single-layer block expected"
    layer = 0
    g_mix = norm_mix_g[layer].reshape(1, D).astype(F32)
    w_in_l = w_in[layer]
    w_qkvu = w_in_l[:, :QKVU_W].astype(BF16)
    w_gate = w_in_l[:, QKVU_W:].astype(BF16)

    qkv, u = _inproj(x, g_mix, w_qkvu, _rope_tables(S))
    attn = _attn(qkv)
    wbu, wc, a, d = _ssm_params(
        ssm_a_re[layer], ssm_a_im[layer], ssm_log_dt[layer], ssm_b_re[layer], ssm_b_im[layer],
        ssm_c_re[layer], ssm_c_im[layer], ssm_d[layer])
    y = _ssm(u, wbu, wc, a, d)
    x1 = _mix(x, attn, y, g_mix, w_gate, w_attn_out[layer].astype(BF16),
              w_glu[layer].astype(BF16), w_out[layer].astype(BF16))
    return _ffn(x1, norm_ffn_g[layer].reshape(1, D).astype(F32),
                norm_final_g.reshape(1, D).astype(F32), w_ffn_gate[layer].astype(BF16),
                w_ffn_up[layer].astype(BF16), w_ffn_down[layer].astype(BF16))
```

```python
import functools

import jax
import jax.numpy as jnp
from jax import lax
from jax.experimental import pallas as pl
from jax.experimental.pallas import tpu as pltpu

F32 = jnp.float32
BF16 = jnp.bfloat16

D_MODEL = 1024
HEAD_DIM = 64
HEADS_PER_GROUP = 4
ATTN_GROUPS = ((128, 1), (512, 4), (2048, 16))
N_ATTN_HEADS = HEADS_PER_GROUP * len(ATTN_GROUPS)
ATTN_OUT_W = HEADS_PER_GROUP * HEAD_DIM
ROPE_DIM = HEAD_DIM // 4
ROPE_THETA = 500000.0
BLOCK = 128
SSM_CH = 16
SSM_GROUPS = 32
SSM_W = SSM_CH * SSM_GROUPS
SSM_STATE = 64
SSM_LANES = SSM_GROUPS * SSM_STATE
D_FF = 2816
QK_W = 2 * N_ATTN_HEADS * HEAD_DIM
QKV_W = 3 * N_ATTN_HEADS * HEAD_DIM
QKVU_W = QKV_W + SSM_W
RMS_EPS = 1e-6
NEG_INF = -1e30

LANE = 128
PERM_ROWS = 256
VMEM_LIMIT = 56 * 1024 * 1024

TS_INPROJ = 512
TM_MIX = 512
TM_FFN = 512
SSM_TC = 32
ATTN_UNROLL = 4
SSM_PAIR = 2
SSM_UNROLL = 8
SUBLANE = 8


def _rmsnorm(x, g):
    return x * lax.rsqrt(jnp.mean(x * x, axis=-1, keepdims=True) + RMS_EPS) * g


def _const_spec(shape):
    return pl.BlockSpec(shape, lambda *_: (0,) * len(shape), pipeline_mode=pl.Buffered(1))


def _inproj_kernel(x_ref, g_ref, w_ref, rc_ref, ra_ref, rb_ref, qkv_ref, u_ref):
    h = _rmsnorm(x_ref[0], g_ref[...]).astype(BF16)
    p = jnp.dot(h, w_ref[...], preferred_element_type=F32)
    rc, ra, rb = rc_ref[...], ra_ref[...], rb_ref[...]
    for c in range(QK_W // LANE):
        t = p[:, c * LANE:(c + 1) * LANE]
        half = ROPE_DIM // 2
        t = t * rc + pltpu.roll(t, LANE - half, 1) * ra + pltpu.roll(t, half, 1) * rb
        qkv_ref[0, :, c * LANE:(c + 1) * LANE] = t.astype(BF16)
    qkv_ref[0, :, QK_W:QKV_W] = p[:, QK_W:QKV_W].astype(BF16)
    u_ref[0] = p[:, QKV_W:QKVU_W]


def _rope_tables(S):
    half = ROPE_DIM // 2
    pos = jnp.arange(S, dtype=F32)
    inv = jnp.power(jnp.float32(ROPE_THETA), -jnp.arange(half, dtype=F32) * 2.0 / ROPE_DIM)
    ang = pos[:, None] * inv[None, :]
    cos, sin = jnp.cos(ang), jnp.sin(ang)
    ones = jnp.ones((S, HEAD_DIM - ROPE_DIM), F32)
    zeros_h = jnp.zeros((S, half), F32)
    zeros_r = jnp.zeros((S, HEAD_DIM - ROPE_DIM), F32)
    per_head_c = jnp.concatenate([cos, cos, ones], axis=1)
    per_head_a = jnp.concatenate([-sin, zeros_h, zeros_r], axis=1)
    per_head_b = jnp.concatenate([zeros_h, sin, zeros_r], axis=1)
    rep = LANE // HEAD_DIM
    return tuple(jnp.tile(t, (1, rep)) for t in (per_head_c, per_head_a, per_head_b))


def _inproj(x, g, w, rope):
    B, S, D = x.shape
    ts = TS_INPROJ
    tab_spec = pl.BlockSpec((ts, LANE), lambda b, i: (i, 0))
    return pl.pallas_call(
        _inproj_kernel,
        grid=(B, S // ts),
        in_specs=[
            pl.BlockSpec((1, ts, D), lambda b, i: (b, i, 0)),
            _const_spec((1, D)),
            _const_spec((D, QKVU_W)),
            tab_spec, tab_spec, tab_spec,
        ],
        out_specs=[
            pl.BlockSpec((1, ts, QKV_W), lambda b, i: (b, i, 0)),
            pl.BlockSpec((1, ts, SSM_W), lambda b, i: (b, i, 0)),
        ],
        out_shape=[
            jax.ShapeDtypeStruct((B, S, QKV_W), BF16),
            jax.ShapeDtypeStruct((B, S, SSM_W), F32),
        ],
        compiler_params=pltpu.CompilerParams(
            dimension_semantics=("arbitrary", "arbitrary"), vmem_limit_bytes=VMEM_LIMIT),
        name="inproj",
    )(x, g, w, *rope)


def _split3(x):
    hi = x.astype(BF16)
    r1 = x - hi.astype(F32)
    mid = r1.astype(BF16)
    lo = (r1 - mid.astype(F32)).astype(BF16)
    return hi, mid, lo


def _head_combine(per_head, lane_lo):
    return jnp.concatenate(
        [jnp.where(lane_lo, per_head[0], per_head[1]), jnp.where(lane_lo, per_head[2], per_head[3])],
        axis=1)


def _attn_block(q, k, v, bias, hm, lane_lo):
    nh = HEADS_PER_GROUP
    qm = jnp.concatenate([q] * nh, axis=0) * hm
    s = lax.dot_general(qm, k, (((1,), (1,)), ((), ())), preferred_element_type=F32)
    s = s + jnp.concatenate([bias] * nh, axis=0)
    m = jnp.max(s, axis=-1, keepdims=True)
    p = jnp.exp(s - m)
    l = jnp.sum(p, axis=-1, keepdims=True)
    pv = jnp.dot(p.astype(BF16), v, preferred_element_type=F32)
    inv_l = 1.0 / l
    lse = m + jnp.log(l)
    rows = [slice(h * BLOCK, (h + 1) * BLOCK) for h in range(nh)]
    o = _head_combine(
        [pv[rows[h], (h // 2) * LANE:(h // 2 + 1) * LANE] * inv_l[rows[h]] for h in range(nh)], lane_lo)
    lse_b = _head_combine([jnp.broadcast_to(lse[rows[h]], (BLOCK, LANE)) for h in range(nh)], lane_lo)
    return o, lse_b


def _attn_kernel(qkv_ref, hm_ref, pf4_ref, pf16_ref, o_ref, qs, ks, vs, o_scr, l_scr, bias_scr):
    S = qkv_ref.shape[1]
    n_chunks = ATTN_OUT_W // LANE
    perms = {4: pf4_ref, 16: pf16_ref}
    lane_lo = lax.broadcasted_iota(jnp.int32, (BLOCK, LANE), 1) < HEAD_DIM
    qi = lax.broadcasted_iota(jnp.int32, (BLOCK, 2 * BLOCK), 0)
    ki = lax.broadcasted_iota(jnp.int32, (BLOCK, 2 * BLOCK), 1)
    dist = qi + BLOCK - ki
    band = (dist >= 0) & (dist <= BLOCK)
    bias_scr[1] = jnp.where(band, 0.0, NEG_INF)
    bias_scr[0] = jnp.where(band & (ki >= BLOCK), 0.0, NEG_INF)
    hm = hm_ref[...]

    for gi, (window, d) in enumerate(ATTN_GROUPS):
        L = S // d
        nb = L // BLOCK
        cols = [part * N_ATTN_HEADS * HEAD_DIM + gi * ATTN_OUT_W for part in range(3)]
        plen = PERM_ROWS // d
        ppb = BLOCK // plen
        if d > 1:
            pf = perms[d][...]
            for scr, col in zip((qs, ks, vs), cols):
                for mblk in range(S // PERM_ROWS):
                    rs = slice(mblk * PERM_ROWS, (mblk + 1) * PERM_ROWS)
                    blk = qkv_ref[0, rs, col:col + ATTN_OUT_W]
                    scr[rs, :] = jnp.dot(pf, blk, preferred_element_type=F32).astype(BF16)

        def piece_rows(r, n, jj, d=d, plen=plen, ppb=ppb):
            if d == 1:
                return pl.ds(pl.multiple_of(n * BLOCK, BLOCK), BLOCK)
            return pl.ds(pl.multiple_of((n * ppb + jj) * PERM_ROWS + r * plen, plen), plen)

        def load(scr, col, r, n, d=d, ppb=ppb, piece_rows=piece_rows):
            if d == 1:
                return qkv_ref[0, piece_rows(r, n, 0), col:col + ATTN_OUT_W]
            return jnp.concatenate([scr[piece_rows(r, n, jj), :] for jj in range(ppb)], axis=0)

        def body(idx, carry, gi=gi, d=d, nb=nb, cols=cols, load=load, plen=plen, ppb=ppb,
                 piece_rows=piece_rows):
            r = idx // nb
            n = idx % nb
            q = load(qs, cols[0], r, n)
            k = load(ks, cols[1], r, n)
            v = load(vs, cols[2], r, n)
            if nb > 1:
                prev = jnp.maximum(n - 1, 0)
                k = jnp.concatenate([load(ks, cols[1], r, prev), k], axis=0)
                v = jnp.concatenate([load(vs, cols[2], r, prev), v], axis=0)
                bias = bias_scr[jnp.minimum(n, 1)]
            else:
                bias = bias_scr[1, :, BLOCK:]
            o, lse_b = _attn_block(q, k, v, bias, hm, lane_lo)
            start = r + d * BLOCK * n
            rows = pl.ds(pl.multiple_of(start, BLOCK), BLOCK) if d == 1 else pl.ds(start, BLOCK, stride=d)
            for c in range(n_chunks):
                o_scr[gi, c, rows, :] = o[:, c * LANE:(c + 1) * LANE]
                l_scr[gi, c, rows, :] = lse_b[:, c * LANE:(c + 1) * LANE]
            return carry

        lax.fori_loop(0, d * nb, body, 0, unroll=ATTN_UNROLL)

    n_groups = len(ATTN_GROUPS)

    def merge(i, carry):
        rs = pl.ds(pl.multiple_of(i * PERM_ROWS, PERM_ROWS), PERM_ROWS)
        for c in range(n_chunks):
            ls = [l_scr[g, c, rs, :] for g in range(n_groups)]
            m = functools.reduce(jnp.maximum, ls)
            es = [jnp.exp(l - m) for l in ls]
            den = functools.reduce(lambda a, b: a + b, es)
            acc = None
            for g in range(n_groups):
                term = (es[g] / den) * o_scr[g, c, rs, :]
                acc = term if acc is None else acc + term
            o_ref[0, rs, c * LANE:(c + 1) * LANE] = acc.astype(o_ref.dtype)
        return carry

    lax.fori_loop(0, S // PERM_ROWS, merge, 0)


def _residue_permutation(d):
    dst = jnp.arange(PERM_ROWS)
    plen = PERM_ROWS // d
    src = (dst % plen) * d + dst // plen
    return (src[:, None] == jnp.arange(PERM_ROWS)[None, :]).astype(BF16)


def _attn(qkv):
    B, S, _ = qkv.shape
    n_groups = len(ATTN_GROUPS)
    for window, d in ATTN_GROUPS:
        assert window // d == BLOCK and S % (d * BLOCK) == 0
        assert d == 1 or (PERM_ROWS % d == 0 and BLOCK % (PERM_ROWS // d) == 0 and S % PERM_ROWS == 0)
    lane_head = jnp.arange(ATTN_OUT_W) // HEAD_DIM
    row_head = jnp.arange(HEADS_PER_GROUP * BLOCK) // BLOCK
    hm = jnp.where(row_head[:, None] == lane_head[None, :], HEAD_DIM ** -0.5, 0.0).astype(BF16)
    n_chunks = ATTN_OUT_W // LANE
    perm_mats = [_residue_permutation(d) for _, d in ATTN_GROUPS if d > 1]
    return pl.pallas_call(
        _attn_kernel,
        grid=(B,),
        in_specs=[pl.BlockSpec((1, S, QKV_W), lambda b: (b, 0, 0)), _const_spec(hm.shape)]
        + [_const_spec(m.shape) for m in perm_mats],
        out_specs=pl.BlockSpec((1, S, ATTN_OUT_W), lambda b: (b, 0, 0)),
        out_shape=jax.ShapeDtypeStruct((B, S, ATTN_OUT_W), BF16),
        scratch_shapes=[
            pltpu.VMEM((S, ATTN_OUT_W), BF16),
            pltpu.VMEM((S, ATTN_OUT_W), BF16),
            pltpu.VMEM((S, ATTN_OUT_W), BF16),
            pltpu.VMEM((n_groups, n_chunks, S, LANE), F32),
            pltpu.VMEM((n_groups, n_chunks, S, LANE), F32),
            pltpu.VMEM((2, BLOCK, 2 * BLOCK), F32),
        ],
        compiler_params=pltpu.CompilerParams(
            dimension_semantics=("arbitrary",), vmem_limit_bytes=VMEM_LIMIT),
        name="attn",
    )(qkv, hm, *perm_mats)


def _ssm_kernel(u_ref, p_ref, pt_ref, wbu_ref, wc_ref, a_ref, d_ref, y_ref, bu_scr, st_scr):
    nb, tc, _ = u_ref.shape
    rows = nb * tc
    n_half = nb // SUBLANE
    hrows = SUBLANE * tc
    n_chunks = SSM_LANES // LANE
    tile_n = 2 * LANE

    @pl.when(pl.program_id(0) == 0)
    def _():
        st_scr[...] = jnp.zeros_like(st_scr)

    def permute(pm, x):
        return jnp.concatenate(
            [jnp.dot(pm, x[h * hrows:(h + 1) * hrows], preferred_element_type=F32)
             for h in range(n_half)], axis=0)

    u_hi, u_mid, u_lo = _split3(u_ref[...].reshape(rows, SSM_W))
    p_fwd = p_ref[...]
    ut_hi = permute(p_fwd, u_hi)
    u = ut_hi + permute(p_fwd, u_mid) + permute(p_fwd, u_lo)
    ub = ut_hi.astype(BF16)
    for i in range(SSM_LANES // tile_n):
        k0 = (i * tile_n // SSM_STATE * SSM_CH) // LANE * LANE
        usl = ub[:, k0:k0 + LANE]
        for ri in range(2):
            bu = jnp.dot(usl, wbu_ref[ri, i], preferred_element_type=F32)
            for c in range(tile_n // LANE):
                bu_scr[ri, i * (tile_n // LANE) + c] = bu[:, c * LANE:(c + 1) * LANE]

    for cp in range(n_chunks // SSM_PAIR):
        chains = [(cp * SSM_PAIR + j, h) for j in range(SSM_PAIR) for h in range(n_half)]
        a_re = {c: a_ref[0, c] for c, _ in chains}
        a_im = {c: a_ref[1, c] for c, _ in chains}
        init = tuple(st_scr[ri, c, h] for c, h in chains for ri in range(2))

        def step(t, carry, chains=chains, a_re=a_re, a_im=a_im):
            out = []
            for j, (c, h) in enumerate(chains):
                xr, xi = carry[2 * j], carry[2 * j + 1]
                row = pl.ds(pl.multiple_of(h * hrows + SUBLANE * t, SUBLANE), SUBLANE)
                nr = a_re[c] * xr - a_im[c] * xi + bu_scr[0, c, row, :]
                ni = a_re[c] * xi + a_im[c] * xr + bu_scr[1, c, row, :]
                bu_scr[0, c, row, :] = nr
                bu_scr[1, c, row, :] = ni
                out += [nr, ni]
            return tuple(out)

        fin = lax.fori_loop(0, tc, step, init, unroll=SSM_UNROLL)
        for j, (c, h) in enumerate(chains):
            st_scr[0, c, h] = fin[2 * j]
            st_scr[1, c, h] = fin[2 * j + 1]

    out_tile = 2 * LANE
    k_chunks = (out_tile // SSM_CH * SSM_STATE) // LANE
    ys = []
    for j in range(SSM_W // out_tile):
        acc = None
        for ri in range(2):
            xs = jnp.concatenate(
                [bu_scr[ri, j * k_chunks + c] for c in range(k_chunks)], axis=1).astype(BF16)
            part = jnp.dot(xs, wc_ref[ri, j], preferred_element_type=F32)
            acc = part if acc is None else acc + part
        ys.append(acc)
    y = jnp.concatenate(ys, axis=1) + d_ref[...] * u
    y = jax.nn.gelu(y).astype(BF16)
    y = permute(pt_ref[...], y).astype(y_ref.dtype)
    y_ref[...] = y.reshape(nb, tc, SSM_W)


def _ssm_params(a_re, a_im, log_dt, b_re, b_im, c_re, c_im, d_skip):
    lr, li = a_re.astype(F32), a_im.astype(F32)
    dt = jnp.exp(log_dt.astype(F32))[:, None]
    mag = jnp.exp(lr * dt)
    ab_re, ab_im = mag * jnp.cos(li * dt), mag * jnp.sin(li * dt)
    den = lr * lr + li * li
    nr, ni = ab_re - 1.0, ab_im
    f_re = (nr * lr + ni * li) / den
    f_im = (ni * lr - nr * li) / den
    br, bi = b_re.astype(F32), b_im.astype(F32)
    bb_re = f_re[..., None] * br - f_im[..., None] * bi
    bb_im = f_re[..., None] * bi + f_im[..., None] * br
    eye = jnp.eye(SSM_GROUPS, dtype=F32)
    tile_n = 2 * LANE

    def bu_tiles(bb):
        full = jnp.einsum('gnc,gh->gchn', bb, eye).reshape(SSM_W, SSM_LANES)
        tiles = []
        for i in range(SSM_LANES // tile_n):
            k0 = (i * tile_n // SSM_STATE * SSM_CH) // LANE * LANE
            tiles.append(full[k0:k0 + LANE, i * tile_n:(i + 1) * tile_n])
        return jnp.stack(tiles)

    def c_tiles(c):
        full = jnp.einsum('gcn,gh->gnhc', c, eye).reshape(SSM_LANES, SSM_W)
        k_rows = tile_n // SSM_CH * SSM_STATE
        return jnp.stack([full[j * k_rows:(j + 1) * k_rows, j * tile_n:(j + 1) * tile_n]
                          for j in range(SSM_W // tile_n)])

    wbu = jnp.stack([bu_tiles(bb_re), bu_tiles(bb_im)]).astype(BF16)
    wc = jnp.stack([c_tiles(c_re.astype(F32)), c_tiles(-c_im.astype(F32))]).astype(BF16)
    n_chunks = SSM_LANES // LANE
    a = jnp.stack([ab_re.reshape(n_chunks, 1, LANE), ab_im.reshape(n_chunks, 1, LANE)])
    a = jnp.broadcast_to(a, (2, n_chunks, SUBLANE, LANE))
    d = d_skip.astype(F32).reshape(1, SSM_W)
    return wbu, wc, a, d


def _ssm_row_permutation(tc):
    dst = jnp.arange(SUBLANE * tc)
    src = (dst % SUBLANE) * tc + dst // SUBLANE
    p = (src[:, None] == jnp.arange(SUBLANE * tc)[None, :]).astype(BF16)
    return p, p.T


def _ssm(u, wbu, wc, a, d):
    B, S, _ = u.shape
    tc = SSM_TC
    n_chunks = SSM_LANES // LANE
    assert B % SUBLANE == 0 and S % tc == 0
    p_fwd, p_bwd = _ssm_row_permutation(tc)
    return pl.pallas_call(
        _ssm_kernel,
        grid=(S // tc,),
        in_specs=[
            pl.BlockSpec((B, tc, SSM_W), lambda i: (0, i, 0)),
            _const_spec(p_fwd.shape),
            _const_spec(p_bwd.shape),
            _const_spec(wbu.shape),
            _const_spec(wc.shape),
            _const_spec(a.shape),
            _const_spec(d.shape),
        ],
        out_specs=pl.BlockSpec((B, tc, SSM_W), lambda i: (0, i, 0)),
        out_shape=jax.ShapeDtypeStruct((B, S, SSM_W), BF16),
        scratch_shapes=[
            pltpu.VMEM((2, n_chunks, B * tc, LANE), F32),
            pltpu.VMEM((2, n_chunks, B // SUBLANE, SUBLANE, LANE), F32),
        ],
        compiler_params=pltpu.CompilerParams(
            dimension_semantics=("arbitrary",), vmem_limit_bytes=VMEM_LIMIT),
        name="ssm",
    )(u, p_fwd, p_bwd, wbu, wc, a, d)


def _mix_kernel(x_ref, attn_ref, y_ref, g_ref, wg_ref, wao_ref, wglu_ref, wout_ref, o_ref):
    x = x_ref[0]
    h = _rmsnorm(x, g_ref[...]).astype(BF16)
    gate = jax.nn.sigmoid(jnp.dot(h, wg_ref[...], preferred_element_type=F32))
    attn_d = jnp.dot(attn_ref[0], wao_ref[...], preferred_element_type=F32)
    z = jnp.dot(y_ref[0], wglu_ref[...], preferred_element_type=F32)
    ssm_out = z[:, :D_MODEL] * jax.nn.sigmoid(z[:, D_MODEL:])
    merged = gate[:, :D_MODEL] * attn_d + gate[:, D_MODEL:] * ssm_out
    o_ref[0] = x + jnp.dot(merged.astype(BF16), wout_ref[...], preferred_element_type=F32)


def _mix(x, attn, y, g, wg, wao, wglu, wout):
    B, S, D = x.shape
    tm = TM_MIX
    tok = lambda w: pl.BlockSpec((1, tm, w), lambda b, i: (b, i, 0))
    return pl.pallas_call(
        _mix_kernel,
        grid=(B, S // tm),
        in_specs=[tok(D), tok(ATTN_OUT_W), tok(SSM_W), _const_spec(g.shape), _const_spec(wg.shape),
                  _const_spec(wao.shape), _const_spec(wglu.shape), _const_spec(wout.shape)],
        out_specs=tok(D),
        out_shape=jax.ShapeDtypeStruct((B, S, D), F32),
        compiler_params=pltpu.CompilerParams(
            dimension_semantics=("arbitrary", "arbitrary"), vmem_limit_bytes=VMEM_LIMIT),
        name="mix",
    )(x, attn, y, g, wg, wao, wglu, wout)


def _ffn_kernel(x_ref, g2_ref, gf_ref, wgate_ref, wup_ref, wdown_ref, o_ref):
    x = x_ref[0]
    h = _rmsnorm(x, g2_ref[...]).astype(BF16)
    gate = jnp.dot(h, wgate_ref[...], preferred_element_type=F32)
    up = jnp.dot(h, wup_ref[...], preferred_element_type=F32)
    act = (jax.nn.silu(gate) * up).astype(BF16)
    x = x + jnp.dot(act, wdown_ref[...], preferred_element_type=F32)
    o_ref[0] = _rmsnorm(x, gf_ref[...])


def _ffn(x, g2, gf, wgate, wup, wdown):
    B, S, D = x.shape
    tm = TM_FFN
    tok = pl.BlockSpec((1, tm, D), lambda b, i: (b, i, 0))
    return pl.pallas_call(
        _ffn_kernel,
        grid=(B, S // tm),
        in_specs=[tok, _const_spec(g2.shape), _const_spec(gf.shape), _const_spec(wgate.shape),
                  _const_spec(wup.shape), _const_spec(wdown.shape)],
        out_specs=tok,
        out_shape=jax.ShapeDtypeStruct((B, S, D), F32),
        compiler_params=pltpu.CompilerParams(
            dimension_semantics=("arbitrary", "arbitrary"), vmem_limit_bytes=VMEM_LIMIT),
        name="ffn",
    )(x, g2, gf, wgate, wup, wdown)


def kernel(x, norm_mix_g, w_in, ssm_a_re, ssm_a_im, ssm_log_dt, ssm_b_re, ssm_b_im, ssm_c_re, ssm_c_im, ssm_d, w_glu, w_attn_out, w_out, norm_ffn_g, w_ffn_gate, w_ffn_up, w_ffn_down, norm_final_g):
    B, S, D = x.shape
    assert D == D_MODEL and norm_mix_g.shape[0] == 1, "single-layer block expected"
    layer = 0
    g_mix = norm_mix_g[layer].reshape(1, D).astype(F32)
    w_in_l = w_in[layer]
    w_qkvu = w_in_l[:, :QKVU_W].astype(BF16)
    w_gate = w_in_l[:, QKVU_W:].astype(BF16)

    qkv, u = _inproj(x, g_mix, w_qkvu, _rope_tables(S))
    attn = _attn(qkv)
    wbu, wc, a, d = _ssm_params(
        ssm_a_re[layer], ssm_a_im[layer], ssm_log_dt[layer], ssm_b_re[layer], ssm_b_im[layer],
        ssm_c_re[layer], ssm_c_im[layer], ssm_d[layer])
    y = _ssm(u, wbu, wc, a, d)
    x1 = _mix(x, attn, y, g_mix, w_gate, w_attn_out[layer].astype(BF16),
              w_glu[layer].astype(BF16), w_out[layer].astype(BF16))
    return _ffn(x1, norm_ffn_g[layer].reshape(1, D).astype(F32),
                norm_final_g.reshape(1, D).astype(F32), w_ffn_gate[layer].astype(BF16),
                w_ffn_up[layer].astype(BF16), w_ffn_down[layer].astype(BF16))
```

```python
import functools

import jax
import jax.numpy as jnp
import numpy as np
from jax import lax
from jax.experimental import pallas as pl
from jax.experimental.pallas import tpu as pltpu

F32 = jnp.float32
BF16 = jnp.bfloat16

D_MODEL = 1024
HEAD_DIM = 64
HEADS_PER_GROUP = 4
ATTN_GROUPS = ((128, 1), (512, 4), (2048, 16))
N_ATTN_HEADS = HEADS_PER_GROUP * len(ATTN_GROUPS)
ATTN_OUT_W = HEADS_PER_GROUP * HEAD_DIM
ROPE_DIM = HEAD_DIM // 4
ROPE_THETA = 500000.0
BLOCK = 128
SSM_CH = 16
SSM_GROUPS = 32
SSM_W = SSM_CH * SSM_GROUPS
SSM_STATE = 64
SSM_LANES = SSM_GROUPS * SSM_STATE
D_FF = 2816
QK_W = 2 * N_ATTN_HEADS * HEAD_DIM
QKV_W = 3 * N_ATTN_HEADS * HEAD_DIM
QKVU_W = QKV_W + SSM_W
RMS_EPS = 1e-6
NEG_INF = -1e30

LANE = 128
PERM_ROWS = 256
VMEM_LIMIT = 56 * 1024 * 1024

TS_INPROJ = 512
TM_MIX = 512
TM_FFN = 512
SSM_TC = 32
ATTN_UNROLL = 4
SSM_PAIR = 2
SSM_UNROLL = 8
SUBLANE = 8


def _rmsnorm(x, g):
    return x * lax.rsqrt(jnp.mean(x * x, axis=-1, keepdims=True) + RMS_EPS) * g


def _const_spec(shape):
    return pl.BlockSpec(shape, lambda *_: (0,) * len(shape), pipeline_mode=pl.Buffered(1))


def _inproj_kernel(x_ref, g_ref, w_ref, rc_ref, ra_ref, rb_ref, qkv_ref, u_ref):
    h = _rmsnorm(x_ref[0], g_ref[...]).astype(BF16)
    p = jnp.dot(h, w_ref[...], preferred_element_type=F32)
    rc, ra, rb = rc_ref[...], ra_ref[...], rb_ref[...]
    for c in range(QK_W // LANE):
        t = p[:, c * LANE:(c + 1) * LANE]
        half = ROPE_DIM // 2
        t = t * rc + pltpu.roll(t, LANE - half, 1) * ra + pltpu.roll(t, half, 1) * rb
        qkv_ref[0, :, c * LANE:(c + 1) * LANE] = t.astype(BF16)
    qkv_ref[0, :, QK_W:QKV_W] = p[:, QK_W:QKV_W].astype(BF16)
    u_ref[0] = p[:, QKV_W:QKVU_W]


def _rope_tables(S):
    half = ROPE_DIM // 2
    pos = np.arange(S, dtype=np.float64)
    inv = np.power(np.float64(ROPE_THETA), -np.arange(half, dtype=np.float64) * 2.0 / ROPE_DIM)
    ang = pos[:, None] * inv[None, :]
    cos, sin = np.cos(ang), np.sin(ang)
    ones = np.ones((S, HEAD_DIM - ROPE_DIM))
    zeros_h = np.zeros((S, half))
    zeros_r = np.zeros((S, HEAD_DIM - ROPE_DIM))
    per_head_c = np.concatenate([cos, cos, ones], axis=1)
    per_head_a = np.concatenate([-sin, zeros_h, zeros_r], axis=1)
    per_head_b = np.concatenate([zeros_h, sin, zeros_r], axis=1)
    rep = LANE // HEAD_DIM
    return tuple(jnp.asarray(np.tile(t, (1, rep)), dtype=F32)
                 for t in (per_head_c, per_head_a, per_head_b))


def _inproj(x, g, w, rope):
    B, S, D = x.shape
    ts = TS_INPROJ
    tab_spec = pl.BlockSpec((ts, LANE), lambda b, i: (i, 0))
    return pl.pallas_call(
        _inproj_kernel,
        grid=(B, S // ts),
        in_specs=[
            pl.BlockSpec((1, ts, D), lambda b, i: (b, i, 0)),
            _const_spec((1, D)),
            _const_spec((D, QKVU_W)),
            tab_spec, tab_spec, tab_spec,
        ],
        out_specs=[
            pl.BlockSpec((1, ts, QKV_W), lambda b, i: (b, i, 0)),
            pl.BlockSpec((1, ts, SSM_W), lambda b, i: (b, i, 0)),
        ],
        out_shape=[
            jax.ShapeDtypeStruct((B, S, QKV_W), BF16),
            jax.ShapeDtypeStruct((B, S, SSM_W), F32),
        ],
        compiler_params=pltpu.CompilerParams(
            dimension_semantics=("arbitrary", "arbitrary"), vmem_limit_bytes=VMEM_LIMIT),
        name="inproj",
    )(x, g, w, *rope)


def _split3(x):
    hi = x.astype(BF16)
    r1 = x - hi.astype(F32)
    mid = r1.astype(BF16)
    lo = (r1 - mid.astype(F32)).astype(BF16)
    return hi, mid, lo


def _head_combine(per_head, lane_lo):
    return jnp.concatenate(
        [jnp.where(lane_lo, per_head[0], per_head[1]), jnp.where(lane_lo, per_head[2], per_head[3])],
        axis=1)


def _attn_block(q, k, v, bias, hm, lane_lo):
    nh = HEADS_PER_GROUP
    qm = jnp.concatenate([q] * nh, axis=0) * hm
    s = lax.dot_general(qm, k, (((1,), (1,)), ((), ())), preferred_element_type=F32)
    s = s + jnp.concatenate([bias] * nh, axis=0)
    m = jnp.max(s, axis=-1, keepdims=True)
    p = jnp.exp(s - m)
    l = jnp.sum(p, axis=-1, keepdims=True)
    pv = jnp.dot(p.astype(BF16), v, preferred_element_type=F32)
    inv_l = 1.0 / l
    lse = m + jnp.log(l)
    rows = [slice(h * BLOCK, (h + 1) * BLOCK) for h in range(nh)]
    o = _head_combine(
        [pv[rows[h], (h // 2) * LANE:(h // 2 + 1) * LANE] * inv_l[rows[h]] for h in range(nh)], lane_lo)
    lse_b = _head_combine([jnp.broadcast_to(lse[rows[h]], (BLOCK, LANE)) for h in range(nh)], lane_lo)
    return o, lse_b


def _attn_kernel(qkv_ref, hm_ref, pf4_ref, pf16_ref, o_ref, qs, ks, vs, o_scr, l_scr, bias_scr):
    S = qkv_ref.shape[1]
    n_chunks = ATTN_OUT_W // LANE
    perms = {4: pf4_ref, 16: pf16_ref}
    lane_lo = lax.broadcasted_iota(jnp.int32, (BLOCK, LANE), 1) < HEAD_DIM
    qi = lax.broadcasted_iota(jnp.int32, (BLOCK, 2 * BLOCK), 0)
    ki = lax.broadcasted_iota(jnp.int32, (BLOCK, 2 * BLOCK), 1)
    dist = qi + BLOCK - ki
    band = (dist >= 0) & (dist <= BLOCK)
    bias_scr[1] = jnp.where(band, 0.0, NEG_INF)
    bias_scr[0] = jnp.where(band & (ki >= BLOCK), 0.0, NEG_INF)
    hm = hm_ref[...]

    for gi, (window, d) in enumerate(ATTN_GROUPS):
        L = S // d
        nb = L // BLOCK
        cols = [part * N_ATTN_HEADS * HEAD_DIM + gi * ATTN_OUT_W for part in range(3)]
        plen = PERM_ROWS // d
        ppb = BLOCK // plen
        if d > 1:
            pf = perms[d][...]
            for scr, col in zip((qs, ks, vs), cols):
                for mblk in range(S // PERM_ROWS):
                    rs = slice(mblk * PERM_ROWS, (mblk + 1) * PERM_ROWS)
                    blk = qkv_ref[0, rs, col:col + ATTN_OUT_W]
                    scr[rs, :] = jnp.dot(pf, blk, preferred_element_type=F32).astype(BF16)

        def piece_rows(r, n, jj, d=d, plen=plen, ppb=ppb):
            if d == 1:
                return pl.ds(pl.multiple_of(n * BLOCK, BLOCK), BLOCK)
            return pl.ds(pl.multiple_of((n * ppb + jj) * PERM_ROWS + r * plen, plen), plen)

        def load(scr, col, r, n, d=d, ppb=ppb, piece_rows=piece_rows):
            if d == 1:
                return qkv_ref[0, piece_rows(r, n, 0), col:col + ATTN_OUT_W]
            return jnp.concatenate([scr[piece_rows(r, n, jj), :] for jj in range(ppb)], axis=0)

        def body(idx, carry, gi=gi, d=d, nb=nb, cols=cols, load=load, plen=plen, ppb=ppb,
                 piece_rows=piece_rows):
            r = idx // nb
            n = idx % nb
            q = load(qs, cols[0], r, n)
            k = load(ks, cols[1], r, n)
            v = load(vs, cols[2], r, n)
            if nb > 1:
                prev = jnp.maximum(n - 1, 0)
                k = jnp.concatenate([load(ks, cols[1], r, prev), k], axis=0)
                v = jnp.concatenate([load(vs, cols[2], r, prev), v], axis=0)
                bias = bias_scr[jnp.minimum(n, 1)]
            else:
                bias = bias_scr[1, :, BLOCK:]
            o, lse_b = _attn_block(q, k, v, bias, hm, lane_lo)
            start = r + d * BLOCK * n
            rows = pl.ds(pl.multiple_of(start, BLOCK), BLOCK) if d == 1 else pl.ds(start, BLOCK, stride=d)
            for c in range(n_chunks):
                o_scr[gi, c, rows, :] = o[:, c * LANE:(c + 1) * LANE]
                l_scr[gi, c, rows, :] = lse_b[:, c * LANE:(c + 1) * LANE]
            return carry

        lax.fori_loop(0, d * nb, body, 0, unroll=ATTN_UNROLL)

    n_groups = len(ATTN_GROUPS)

    def merge(i, carry):
        rs = pl.ds(pl.multiple_of(i * PERM_ROWS, PERM_ROWS), PERM_ROWS)
        for c in range(n_chunks):
            ls = [l_scr[g, c, rs, :] for g in range(n_groups)]
            m = functools.reduce(jnp.maximum, ls)
            es = [jnp.exp(l - m) for l in ls]
            den = functools.reduce(lambda a, b: a + b, es)
            acc = None
            for g in range(n_groups):
                term = (es[g] / den) * o_scr[g, c, rs, :]
                acc = term if acc is None else acc + term
            o_ref[0, rs, c * LANE:(c + 1) * LANE] = acc.astype(o_ref.dtype)
        return carry

    lax.fori_loop(0, S // PERM_ROWS, merge, 0)


def _residue_permutation(d):
    dst = np.arange(PERM_ROWS)
    plen = PERM_ROWS // d
    src = (dst % plen) * d + dst // plen
    return jnp.asarray(src[:, None] == np.arange(PERM_ROWS)[None, :], dtype=BF16)


def _attn(qkv):
    B, S, _ = qkv.shape
    n_groups = len(ATTN_GROUPS)
    for window, d in ATTN_GROUPS:
        assert window // d == BLOCK and S % (d * BLOCK) == 0
        assert d == 1 or (PERM_ROWS % d == 0 and BLOCK % (PERM_ROWS // d) == 0 and S % PERM_ROWS == 0)
    lane_head = np.arange(ATTN_OUT_W) // HEAD_DIM
    row_head = np.arange(HEADS_PER_GROUP * BLOCK) // BLOCK
    hm = jnp.asarray(np.where(row_head[:, None] == lane_head[None, :], HEAD_DIM ** -0.5, 0.0), dtype=BF16)
    n_chunks = ATTN_OUT_W // LANE
    perm_mats = [_residue_permutation(d) for _, d in ATTN_GROUPS if d > 1]
    return pl.pallas_call(
        _attn_kernel,
        grid=(B,),
        in_specs=[pl.BlockSpec((1, S, QKV_W), lambda b: (b, 0, 0)), _const_spec(hm.shape)]
        + [_const_spec(m.shape) for m in perm_mats],
        out_specs=pl.BlockSpec((1, S, ATTN_OUT_W), lambda b: (b, 0, 0)),
        out_shape=jax.ShapeDtypeStruct((B, S, ATTN_OUT_W), BF16),
        scratch_shapes=[
            pltpu.VMEM((S, ATTN_OUT_W), BF16),
            pltpu.VMEM((S, ATTN_OUT_W), BF16),
            pltpu.VMEM((S, ATTN_OUT_W), BF16),
            pltpu.VMEM((n_groups, n_chunks, S, LANE), F32),
            pltpu.VMEM((n_groups, n_chunks, S, LANE), F32),
            pltpu.VMEM((2, BLOCK, 2 * BLOCK), F32),
        ],
        compiler_params=pltpu.CompilerParams(
            dimension_semantics=("arbitrary",), vmem_limit_bytes=VMEM_LIMIT),
        name="attn",
    )(qkv, hm, *perm_mats)


def _ssm_kernel(u_ref, p_ref, pt_ref, wbu_ref, wc_ref, a_ref, d_ref, y_ref,
                bu_scr, x_scr, st_scr, ub_scr, us_scr):
    nb, tc, _ = u_ref.shape
    rows = nb * tc
    n_half = nb // SUBLANE
    hrows = SUBLANE * tc
    n_chunks = SSM_LANES // LANE
    tile_n = 2 * LANE
    assert SSM_PAIR * LANE == tile_n, "one scan pass covers the state lanes of one matmul tile"

    @pl.when(pl.program_id(0) == 0)
    def _():
        st_scr[...] = jnp.zeros_like(st_scr)
        bu_scr[...] = jnp.zeros_like(bu_scr)
        us_scr[1] = jnp.zeros(us_scr.shape[1:], F32)

    us_scr[0] = us_scr[1]

    def permute(pm, x):
        return jnp.concatenate(
            [jnp.dot(pm, x[h * hrows:(h + 1) * hrows], preferred_element_type=F32)
             for h in range(n_half)], axis=0)

    u_hi, u_mid, u_lo = _split3(u_ref[...].reshape(rows, SSM_W))
    p_fwd = p_ref[...]
    ut_hi = permute(p_fwd, u_hi)
    us_scr[1] = ut_hi + permute(p_fwd, u_mid) + permute(p_fwd, u_lo)
    ub = ut_hi.astype(BF16)
    for s in range(SSM_W // LANE):
        ub_scr[s] = ub[:, s * LANE:(s + 1) * LANE]

    for cp in range(n_chunks // SSM_PAIR):
        chains = [(cp * SSM_PAIR + j, h) for j in range(SSM_PAIR) for h in range(n_half)]
        a_re = {c: a_ref[0, c] for c, _ in chains}
        a_im = {c: a_ref[1, c] for c, _ in chains}
        carry = [st_scr[ri, c, h] for c, h in chains for ri in range(2)]
        pack = 4 // x_scr.dtype.itemsize
        for t0 in range(0, tc, pack):
            for j, (c, h) in enumerate(chains):
                xr, xi = carry[2 * j], carry[2 * j + 1]
                outs_r, outs_i = [], []
                for t in range(t0, t0 + pack):
                    row = pl.ds(h * hrows + SUBLANE * t, SUBLANE)
                    xr, xi = (a_re[c] * xr - a_im[c] * xi + bu_scr[0, c, row, :],
                              a_re[c] * xi + a_im[c] * xr + bu_scr[1, c, row, :])
                    outs_r.append(xr)
                    outs_i.append(xi)
                rows_p = pl.ds(h * hrows + SUBLANE * t0, SUBLANE * pack)
                x_scr[0, c, rows_p, :] = jnp.concatenate(outs_r, axis=0).astype(x_scr.dtype)
                x_scr[1, c, rows_p, :] = jnp.concatenate(outs_i, axis=0).astype(x_scr.dtype)
                carry[2 * j], carry[2 * j + 1] = xr, xi
        for j, (c, h) in enumerate(chains):
            st_scr[0, c, h] = carry[2 * j]
            st_scr[1, c, h] = carry[2 * j + 1]
        u_slice = (cp * tile_n // SSM_STATE * SSM_CH) // LANE
        for ri in range(2):
            bu = jnp.dot(ub_scr[u_slice], wbu_ref[ri, cp], preferred_element_type=F32)
            for c in range(SSM_PAIR):
                bu_scr[ri, cp * SSM_PAIR + c] = bu[:, c * LANE:(c + 1) * LANE]

    out_tile = 2 * LANE
    k_chunks = (out_tile // SSM_CH * SSM_STATE) // LANE
    ys = []
    for j in range(SSM_W // out_tile):
        acc = None
        for ri in range(2):
            xs = jnp.concatenate(
                [x_scr[ri, j * k_chunks + c] for c in range(k_chunks)], axis=1)
            part = jnp.dot(xs, wc_ref[ri, j], preferred_element_type=F32)
            acc = part if acc is None else acc + part
        ys.append(acc)
    y = jnp.concatenate(ys, axis=1) + d_ref[...] * us_scr[0]
    y = jax.nn.gelu(y).astype(BF16)
    y = permute(pt_ref[...], y).astype(y_ref.dtype)
    y_ref[...] = y.reshape(nb, tc, SSM_W)


def _ssm_params(a_re, a_im, log_dt, b_re, b_im, c_re, c_im, d_skip):
    lr, li = a_re.astype(F32), a_im.astype(F32)
    dt = jnp.exp(log_dt.astype(F32))[:, None]
    mag = jnp.exp(lr * dt)
    ab_re, ab_im = mag * jnp.cos(li * dt), mag * jnp.sin(li * dt)
    den = lr * lr + li * li
    nr, ni = ab_re - 1.0, ab_im
    f_re = (nr * lr + ni * li) / den
    f_im = (ni * lr - nr * li) / den
    br, bi = b_re.astype(F32), b_im.astype(F32)
    bb_re = f_re[..., None] * br - f_im[..., None] * bi
    bb_im = f_re[..., None] * bi + f_im[..., None] * br
    tile_n = 2 * LANE
    n_bu = SSM_LANES // tile_n
    n_c = SSM_W // tile_n
    k_rows = tile_n // SSM_CH * SSM_STATE

    def bu_tiles(bb):
        rows = jnp.transpose(bb, (0, 2, 1)).reshape(SSM_W // LANE, LANE, SSM_STATE)
        k_slice = (np.arange(n_bu) * tile_n // SSM_STATE * SSM_CH) // LANE
        tiled = jnp.tile(rows[k_slice], (1, 1, tile_n // SSM_STATE))
        g_row = k_slice[:, None, None] * (LANE // SSM_CH) + np.arange(LANE)[None, :, None] // SSM_CH
        g_col = (np.arange(n_bu)[:, None, None] * tile_n + np.arange(tile_n)[None, None, :]) // SSM_STATE
        return jnp.where(g_row == g_col, tiled, 0.0)

    def c_tiles(c):
        rows = jnp.transpose(c, (0, 2, 1)).reshape(n_c, k_rows, SSM_CH)
        tiled = jnp.tile(rows, (1, 1, tile_n // SSM_CH))
        same_group = (np.arange(k_rows)[:, None] // SSM_STATE) == (np.arange(tile_n)[None, :] // SSM_CH)
        return jnp.where(same_group[None], tiled, 0.0)

    wbu = jnp.stack([bu_tiles(bb_re), bu_tiles(bb_im)]).astype(BF16)
    wc = jnp.stack([c_tiles(c_re.astype(F32)), c_tiles(-c_im.astype(F32))]).astype(BF16)
    n_chunks = SSM_LANES // LANE
    a = jnp.stack([ab_re.reshape(n_chunks, 1, LANE), ab_im.reshape(n_chunks, 1, LANE)])
    a = jnp.broadcast_to(a, (2, n_chunks, SUBLANE, LANE))
    d = d_skip.astype(F32).reshape(1, SSM_W)
    return wbu, wc, a, d


def _ssm_row_permutation(tc):
    dst = np.arange(SUBLANE * tc)
    src = (dst % SUBLANE) * tc + dst // SUBLANE
    p = src[:, None] == np.arange(SUBLANE * tc)[None, :]
    return jnp.asarray(p, dtype=BF16), jnp.asarray(p.T, dtype=BF16)


def _ssm(u, wbu, wc, a, d):
    B, S, _ = u.shape
    tc = SSM_TC
    n_chunks = SSM_LANES // LANE
    assert B % SUBLANE == 0 and S % tc == 0 and tc % (2 * SSM_UNROLL) == 0
    p_fwd, p_bwd = _ssm_row_permutation(tc)
    n_steps = S // tc
    return pl.pallas_call(
        _ssm_kernel,
        grid=(n_steps + 1,),
        in_specs=[
            pl.BlockSpec((B, tc, SSM_W), lambda i: (0, jnp.minimum(i, n_steps - 1), 0)),
            _const_spec(p_fwd.shape),
            _const_spec(p_bwd.shape),
            _const_spec(wbu.shape),
            _const_spec(wc.shape),
            _const_spec(a.shape),
            _const_spec(d.shape),
        ],
        out_specs=pl.BlockSpec((B, tc, SSM_W), lambda i: (0, jnp.maximum(i - 1, 0), 0)),
        out_shape=jax.ShapeDtypeStruct((B, S, SSM_W), BF16),
        scratch_shapes=[
            pltpu.VMEM((2, n_chunks, B * tc, LANE), F32),
            pltpu.VMEM((2, n_chunks, B * tc, LANE), BF16),
            pltpu.VMEM((2, n_chunks, B // SUBLANE, SUBLANE, LANE), F32),
            pltpu.VMEM((SSM_W // LANE, B * tc, LANE), BF16),
            pltpu.VMEM((2, B * tc, SSM_W), F32),
        ],
        compiler_params=pltpu.CompilerParams(
            dimension_semantics=("arbitrary",), vmem_limit_bytes=VMEM_LIMIT),
        name="ssm",
    )(u, p_fwd, p_bwd, wbu, wc, a, d)


def _mix_kernel(x_ref, attn_ref, y_ref, g_ref, wg_ref, wao_ref, wglu_ref, wout_ref, o_ref):
    x = x_ref[0]
    h = _rmsnorm(x, g_ref[...]).astype(BF16)
    gate = jax.nn.sigmoid(jnp.dot(h, wg_ref[...], preferred_element_type=F32))
    attn_d = jnp.dot(attn_ref[0], wao_ref[...], preferred_element_type=F32)
    z = jnp.dot(y_ref[0], wglu_ref[...], preferred_element_type=F32)
    ssm_out = z[:, :D_MODEL] * jax.nn.sigmoid(z[:, D_MODEL:])
    merged = gate[:, :D_MODEL] * attn_d + gate[:, D_MODEL:] * ssm_out
    o_ref[0] = x + jnp.dot(merged.astype(BF16), wout_ref[...], preferred_element_type=F32)


def _mix(x, attn, y, g, wg, wao, wglu, wout):
    B, S, D = x.shape
    tm = TM_MIX
    tok = lambda w: pl.BlockSpec((1, tm, w), lambda b, i: (b, i, 0))
    return pl.pallas_call(
        _mix_kernel,
        grid=(B, S // tm),
        in_specs=[tok(D), tok(ATTN_OUT_W), tok(SSM_W), _const_spec(g.shape), _const_spec(wg.shape),
                  _const_spec(wao.shape), _const_spec(wglu.shape), _const_spec(wout.shape)],
        out_specs=tok(D),
        out_shape=jax.ShapeDtypeStruct((B, S, D), F32),
        compiler_params=pltpu.CompilerParams(
            dimension_semantics=("arbitrary", "arbitrary"), vmem_limit_bytes=VMEM_LIMIT),
        name="mix",
    )(x, attn, y, g, wg, wao, wglu, wout)


def _ffn_kernel(x_ref, g2_ref, gf_ref, wgate_ref, wup_ref, wdown_ref, o_ref):
    x = x_ref[0]
    h = _rmsnorm(x, g2_ref[...]).astype(BF16)
    gate = jnp.dot(h, wgate_ref[...], preferred_element_type=F32)
    up = jnp.dot(h, wup_ref[...], preferred_element_type=F32)
    act = (jax.nn.silu(gate) * up).astype(BF16)
    x = x + jnp.dot(act, wdown_ref[...], preferred_element_type=F32)
    o_ref[0] = _rmsnorm(x, gf_ref[...])


def _ffn(x, g2, gf, wgate, wup, wdown):
    B, S, D = x.shape
    tm = TM_FFN
    tok = pl.BlockSpec((1, tm, D), lambda b, i: (b, i, 0))
    return pl.pallas_call(
        _ffn_kernel,
        grid=(B, S // tm),
        in_specs=[tok, _const_spec(g2.shape), _const_spec(gf.shape), _const_spec(wgate.shape),
                  _const_spec(wup.shape), _const_spec(wdown.shape)],
        out_specs=tok,
        out_shape=jax.ShapeDtypeStruct((B, S, D), F32),
        compiler_params=pltpu.CompilerParams(
            dimension_semantics=("arbitrary", "arbitrary"), vmem_limit_bytes=VMEM_LIMIT),
        name="ffn",
    )(x, g2, gf, wgate, wup, wdown)


def kernel(x, norm_mix_g, w_in, ssm_a_re, ssm_a_im, ssm_log_dt, ssm_b_re, ssm_b_im, ssm_c_re, ssm_c_im, ssm_d, w_glu, w_attn_out, w_out, norm_ffn_g, w_ffn_gate, w_ffn_up, w_ffn_down, norm_final_g):
    B, S, D = x.shape
    assert D == D_MODEL and norm_mix_g.shape[0] == 1, "single-layer block expected"
    layer = 0
    g_mix = norm_mix_g[layer].reshape(1, D).astype(F32)
    w_in_l = w_in[layer]
    w_qkvu = w_in_l[:, :QKVU_W].astype(BF16)
    w_gate = w_in_l[:, QKVU_W:].astype(BF16)

    qkv, u = _inproj(x, g_mix, w_qkvu, _rope_tables(S))
    attn = _attn(qkv)
    wbu, wc, a, d = _ssm_params(
        ssm_a_re[layer], ssm_a_im[layer], ssm_log_dt[layer], ssm_b_re[layer], ssm_b_im[layer],
        ssm_c_re[layer], ssm_c_im[layer], ssm_d[layer])
    y = _ssm(u, wbu, wc, a, d)
    x1 = _mix(x, attn, y, g_mix, w_gate, w_attn_out[layer].astype(BF16),
              w_glu[layer].astype(BF16), w_out[layer].astype(BF16))
    return _ffn(x1, norm_ffn_g[layer].reshape(1, D).astype(F32),
                norm_final_g.reshape(1, D).astype(F32), w_ffn_gate[layer].astype(BF16),
                w_ffn_up[layer].astype(BF16), w_ffn_down[layer].astype(BF16))
```

```python
import functools

import jax
import jax.numpy as jnp
import numpy as np
from jax import lax
from jax.experimental import pallas as pl
from jax.experimental.pallas import tpu as pltpu

F32 = jnp.float32
BF16 = jnp.bfloat16

D_MODEL = 1024
HEAD_DIM = 64
HEADS_PER_GROUP = 4
ATTN_GROUPS = ((128, 1), (512, 4), (2048, 16))
N_ATTN_HEADS = HEADS_PER_GROUP * len(ATTN_GROUPS)
ATTN_OUT_W = HEADS_PER_GROUP * HEAD_DIM
ROPE_DIM = HEAD_DIM // 4
ROPE_THETA = 500000.0
BLOCK = 128
SSM_CH = 16
SSM_GROUPS = 32
SSM_W = SSM_CH * SSM_GROUPS
SSM_STATE = 64
SSM_LANES = SSM_GROUPS * SSM_STATE
D_FF = 2816
QK_W = 2 * N_ATTN_HEADS * HEAD_DIM
QKV_W = 3 * N_ATTN_HEADS * HEAD_DIM
QKVU_W = QKV_W + SSM_W
RMS_EPS = 1e-6
NEG_INF = -1e30

LANE = 128
PERM_ROWS = 256
VMEM_LIMIT = 56 * 1024 * 1024

TS_INPROJ = 1024
TM_MIX = 1024
TM_FFN = 1024
FFN_CHUNKS = ((0, 1536), (1536, D_FF))
SSM_TC = 32
ATTN_UNROLL = 4
SSM_PAIR = 2
SSM_UNROLL = 8
SUBLANE = 8


def _rmsnorm(x, g):
    return x * lax.rsqrt(jnp.mean(x * x, axis=-1, keepdims=True) + RMS_EPS) * g


def _const_spec(shape):
    return pl.BlockSpec(shape, lambda *_: (0,) * len(shape), pipeline_mode=pl.Buffered(1))


def _inproj_kernel(x_ref, g_ref, w_ref, rc_ref, ra_ref, rb_ref, qkv_ref, u_ref):
    h = _rmsnorm(x_ref[0], g_ref[...]).astype(BF16)
    p = jnp.dot(h, w_ref[...], preferred_element_type=F32)
    rc, ra, rb = rc_ref[...], ra_ref[...], rb_ref[...]
    for c in range(QK_W // LANE):
        t = p[:, c * LANE:(c + 1) * LANE]
        half = ROPE_DIM // 2
        t = t * rc + pltpu.roll(t, LANE - half, 1) * ra + pltpu.roll(t, half, 1) * rb
        qkv_ref[0, :, c * LANE:(c + 1) * LANE] = t.astype(BF16)
    qkv_ref[0, :, QK_W:QKV_W] = p[:, QK_W:QKV_W].astype(BF16)
    u_ref[0] = p[:, QKV_W:QKVU_W]


def _rope_tables(S):
    half = ROPE_DIM // 2
    pos = np.arange(S, dtype=np.float64)
    inv = np.power(np.float64(ROPE_THETA), -np.arange(half, dtype=np.float64) * 2.0 / ROPE_DIM)
    ang = pos[:, None] * inv[None, :]
    cos, sin = np.cos(ang), np.sin(ang)
    ones = np.ones((S, HEAD_DIM - ROPE_DIM))
    zeros_h = np.zeros((S, half))
    zeros_r = np.zeros((S, HEAD_DIM - ROPE_DIM))
    per_head_c = np.concatenate([cos, cos, ones], axis=1)
    per_head_a = np.concatenate([-sin, zeros_h, zeros_r], axis=1)
    per_head_b = np.concatenate([zeros_h, sin, zeros_r], axis=1)
    rep = LANE // HEAD_DIM
    return tuple(jnp.asarray(np.tile(t, (1, rep)), dtype=F32)
                 for t in (per_head_c, per_head_a, per_head_b))


def _inproj(x, g, w, rope):
    B, S, D = x.shape
    ts = TS_INPROJ
    tab_spec = pl.BlockSpec((ts, LANE), lambda b, i: (i, 0))
    return pl.pallas_call(
        _inproj_kernel,
        grid=(B, S // ts),
        in_specs=[
            pl.BlockSpec((1, ts, D), lambda b, i: (b, i, 0)),
            _const_spec((1, D)),
            _const_spec((D, QKVU_W)),
            tab_spec, tab_spec, tab_spec,
        ],
        out_specs=[
            pl.BlockSpec((1, ts, QKV_W), lambda b, i: (b, i, 0)),
            pl.BlockSpec((1, ts, SSM_W), lambda b, i: (b, i, 0)),
        ],
        out_shape=[
            jax.ShapeDtypeStruct((B, S, QKV_W), BF16),
            jax.ShapeDtypeStruct((B, S, SSM_W), F32),
        ],
        compiler_params=pltpu.CompilerParams(
            dimension_semantics=("arbitrary", "arbitrary"), vmem_limit_bytes=VMEM_LIMIT),
        name="inproj",
    )(x, g, w, *rope)


def _split3(x):
    hi = x.astype(BF16)
    r1 = x - hi.astype(F32)
    mid = r1.astype(BF16)
    lo = (r1 - mid.astype(F32)).astype(BF16)
    return hi, mid, lo


def _head_combine(per_head, lane_lo):
    return jnp.concatenate(
        [jnp.where(lane_lo, per_head[0], per_head[1]), jnp.where(lane_lo, per_head[2], per_head[3])],
        axis=1)


def _attn_block(q, k, v, bias, hm, lane_lo):
    nh = HEADS_PER_GROUP
    qm = jnp.concatenate([q] * nh, axis=0) * hm
    s = lax.dot_general(qm, k, (((1,), (1,)), ((), ())), preferred_element_type=F32)
    s = s + jnp.concatenate([bias] * nh, axis=0)
    m = jnp.max(s, axis=-1, keepdims=True)
    p = jnp.exp(s - m)
    l = jnp.sum(p, axis=-1, keepdims=True)
    pv = jnp.dot(p.astype(BF16), v, preferred_element_type=F32)
    inv_l = 1.0 / l
    lse = m + jnp.log(l)
    rows = [slice(h * BLOCK, (h + 1) * BLOCK) for h in range(nh)]
    o = _head_combine(
        [pv[rows[h], (h // 2) * LANE:(h // 2 + 1) * LANE] * inv_l[rows[h]] for h in range(nh)], lane_lo)
    lse_b = _head_combine([jnp.broadcast_to(lse[rows[h]], (BLOCK, LANE)) for h in range(nh)], lane_lo)
    return o, lse_b


def _attn_kernel(qkv_ref, hm_ref, pf4_ref, pf16_ref, o_ref, qs, ks, vs, o_scr, l_scr, bias_scr):
    S = qkv_ref.shape[1]
    n_chunks = ATTN_OUT_W // LANE
    perms = {4: pf4_ref, 16: pf16_ref}
    lane_lo = lax.broadcasted_iota(jnp.int32, (BLOCK, LANE), 1) < HEAD_DIM
    qi = lax.broadcasted_iota(jnp.int32, (BLOCK, 2 * BLOCK), 0)
    ki = lax.broadcasted_iota(jnp.int32, (BLOCK, 2 * BLOCK), 1)
    dist = qi + BLOCK - ki
    band = (dist >= 0) & (dist <= BLOCK)
    bias_scr[1] = jnp.where(band, 0.0, NEG_INF)
    bias_scr[0] = jnp.where(band & (ki >= BLOCK), 0.0, NEG_INF)
    hm = hm_ref[...]

    for gi, (window, d) in enumerate(ATTN_GROUPS):
        L = S // d
        nb = L // BLOCK
        cols = [part * N_ATTN_HEADS * HEAD_DIM + gi * ATTN_OUT_W for part in range(3)]
        plen = PERM_ROWS // d
        ppb = BLOCK // plen
        if d > 1:
            pf = perms[d][...]
            for scr, col in zip((qs, ks, vs), cols):
                for mblk in range(S // PERM_ROWS):
                    rs = slice(mblk * PERM_ROWS, (mblk + 1) * PERM_ROWS)
                    blk = qkv_ref[0, rs, col:col + ATTN_OUT_W]
                    scr[rs, :] = jnp.dot(pf, blk, preferred_element_type=F32).astype(BF16)

        def piece_rows(r, n, jj, d=d, plen=plen, ppb=ppb):
            if d == 1:
                return pl.ds(pl.multiple_of(n * BLOCK, BLOCK), BLOCK)
            return pl.ds(pl.multiple_of((n * ppb + jj) * PERM_ROWS + r * plen, plen), plen)

        def load(scr, col, r, n, d=d, ppb=ppb, piece_rows=piece_rows):
            if d == 1:
                return qkv_ref[0, piece_rows(r, n, 0), col:col + ATTN_OUT_W]
            return jnp.concatenate([scr[piece_rows(r, n, jj), :] for jj in range(ppb)], axis=0)

        def body(idx, carry, gi=gi, d=d, nb=nb, cols=cols, load=load, plen=plen, ppb=ppb,
                 piece_rows=piece_rows):
            r = idx // nb
            n = idx % nb
            q = load(qs, cols[0], r, n)
            k = load(ks, cols[1], r, n)
            v = load(vs, cols[2], r, n)
            if nb > 1:
                prev = jnp.maximum(n - 1, 0)
                k = jnp.concatenate([load(ks, cols[1], r, prev), k], axis=0)
                v = jnp.concatenate([load(vs, cols[2], r, prev), v], axis=0)
                bias = bias_scr[jnp.minimum(n, 1)]
            else:
                bias = bias_scr[1, :, BLOCK:]
            o, lse_b = _attn_block(q, k, v, bias, hm, lane_lo)
            start = r + d * BLOCK * n
            rows = pl.ds(pl.multiple_of(start, BLOCK), BLOCK) if d == 1 else pl.ds(start, BLOCK, stride=d)
            for c in range(n_chunks):
                o_scr[gi, c, rows, :] = o[:, c * LANE:(c + 1) * LANE]
                l_scr[gi, c, rows, :] = lse_b[:, c * LANE:(c + 1) * LANE]
            return carry

        lax.fori_loop(0, d * nb, body, 0, unroll=ATTN_UNROLL)

    n_groups = len(ATTN_GROUPS)

    def merge(i, carry):
        rs = pl.ds(pl.multiple_of(i * PERM_ROWS, PERM_ROWS), PERM_ROWS)
        for c in range(n_chunks):
            ls = [l_scr[g, c, rs, :] for g in range(n_groups)]
            m = functools.reduce(jnp.maximum, ls)
            es = [jnp.exp(l - m) for l in ls]
            inv_den = 1.0 / functools.reduce(lambda a, b: a + b, es)
            acc = None
            for g in range(n_groups):
                term = (es[g] * inv_den) * o_scr[g, c, rs, :]
                acc = term if acc is None else acc + term
            o_ref[0, rs, c * LANE:(c + 1) * LANE] = acc.astype(o_ref.dtype)
        return carry

    lax.fori_loop(0, S // PERM_ROWS, merge, 0)


def _residue_permutation(d):
    dst = np.arange(PERM_ROWS)
    plen = PERM_ROWS // d
    src = (dst % plen) * d + dst // plen
    return jnp.asarray(src[:, None] == np.arange(PERM_ROWS)[None, :], dtype=BF16)


def _attn(qkv):
    B, S, _ = qkv.shape
    n_groups = len(ATTN_GROUPS)
    for window, d in ATTN_GROUPS:
        assert window // d == BLOCK and S % (d * BLOCK) == 0
        assert d == 1 or (PERM_ROWS % d == 0 and BLOCK % (PERM_ROWS // d) == 0 and S % PERM_ROWS == 0)
    lane_head = np.arange(ATTN_OUT_W) // HEAD_DIM
    row_head = np.arange(HEADS_PER_GROUP * BLOCK) // BLOCK
    hm = jnp.asarray(np.where(row_head[:, None] == lane_head[None, :], HEAD_DIM ** -0.5, 0.0), dtype=BF16)
    n_chunks = ATTN_OUT_W // LANE
    perm_mats = [_residue_permutation(d) for _, d in ATTN_GROUPS if d > 1]
    return pl.pallas_call(
        _attn_kernel,
        grid=(B,),
        in_specs=[pl.BlockSpec((1, S, QKV_W), lambda b: (b, 0, 0)), _const_spec(hm.shape)]
        + [_const_spec(m.shape) for m in perm_mats],
        out_specs=pl.BlockSpec((1, S, ATTN_OUT_W), lambda b: (b, 0, 0)),
        out_shape=jax.ShapeDtypeStruct((B, S, ATTN_OUT_W), BF16),
        scratch_shapes=[
            pltpu.VMEM((S, ATTN_OUT_W), BF16),
            pltpu.VMEM((S, ATTN_OUT_W), BF16),
            pltpu.VMEM((S, ATTN_OUT_W), BF16),
            pltpu.VMEM((n_groups, n_chunks, S, LANE), F32),
            pltpu.VMEM((n_groups, n_chunks, S, LANE), F32),
            pltpu.VMEM((2, BLOCK, 2 * BLOCK), F32),
        ],
        compiler_params=pltpu.CompilerParams(
            dimension_semantics=("arbitrary",), vmem_limit_bytes=VMEM_LIMIT),
        name="attn",
    )(qkv, hm, *perm_mats)


def _ssm_kernel(u_ref, p_ref, pt_ref, wbu_ref, wc_ref, a_ref, d_ref, y_ref,
                bu_scr, x_scr, st_scr, ub_scr, us_scr):
    nb, tc, _ = u_ref.shape
    rows = nb * tc
    n_half = nb // SUBLANE
    hrows = SUBLANE * tc
    n_chunks = SSM_LANES // LANE
    tile_n = 2 * LANE
    assert SSM_PAIR * LANE == tile_n, "one scan pass covers the state lanes of one matmul tile"

    @pl.when(pl.program_id(0) == 0)
    def _():
        st_scr[...] = jnp.zeros_like(st_scr)
        bu_scr[...] = jnp.zeros_like(bu_scr)
        us_scr[1] = jnp.zeros(us_scr.shape[1:], F32)

    us_scr[0] = us_scr[1]

    def permute(pm, x):
        return jnp.concatenate(
            [jnp.dot(pm, x[h * hrows:(h + 1) * hrows], preferred_element_type=F32)
             for h in range(n_half)], axis=0)

    u_hi, u_mid, u_lo = _split3(u_ref[...].reshape(rows, SSM_W))
    p_fwd = p_ref[...]
    ut_hi = permute(p_fwd, u_hi)
    us_scr[1] = ut_hi + permute(p_fwd, u_mid) + permute(p_fwd, u_lo)
    ub = ut_hi.astype(BF16)
    for s in range(SSM_W // LANE):
        ub_scr[s] = ub[:, s * LANE:(s + 1) * LANE]

    for cp in range(n_chunks // SSM_PAIR):
        chains = [(cp * SSM_PAIR + j, h) for j in range(SSM_PAIR) for h in range(n_half)]
        a_re = {c: a_ref[0, c] for c, _ in chains}
        a_im = {c: a_ref[1, c] for c, _ in chains}
        carry = [st_scr[ri, c, h] for c, h in chains for ri in range(2)]
        pack = 4 // x_scr.dtype.itemsize
        for t0 in range(0, tc, pack):
            for j, (c, h) in enumerate(chains):
                xr, xi = carry[2 * j], carry[2 * j + 1]
                outs_r, outs_i = [], []
                for t in range(t0, t0 + pack):
                    row = pl.ds(h * hrows + SUBLANE * t, SUBLANE)
                    xr, xi = (a_re[c] * xr - a_im[c] * xi + bu_scr[0, c, row, :],
                              a_re[c] * xi + a_im[c] * xr + bu_scr[1, c, row, :])
                    outs_r.append(xr)
                    outs_i.append(xi)
                rows_p = pl.ds(h * hrows + SUBLANE * t0, SUBLANE * pack)
                x_scr[0, c, rows_p, :] = jnp.concatenate(outs_r, axis=0).astype(x_scr.dtype)
                x_scr[1, c, rows_p, :] = jnp.concatenate(outs_i, axis=0).astype(x_scr.dtype)
                carry[2 * j], carry[2 * j + 1] = xr, xi
        for j, (c, h) in enumerate(chains):
            st_scr[0, c, h] = carry[2 * j]
            st_scr[1, c, h] = carry[2 * j + 1]
        u_slice = (cp * tile_n // SSM_STATE * SSM_CH) // LANE
        for ri in range(2):
            bu = jnp.dot(ub_scr[u_slice], wbu_ref[ri, cp], preferred_element_type=F32)
            for c in range(SSM_PAIR):
                bu_scr[ri, cp * SSM_PAIR + c] = bu[:, c * LANE:(c + 1) * LANE]

    out_tile = 2 * LANE
    k_chunks = (out_tile // SSM_CH * SSM_STATE) // LANE
    ys = []
    for j in range(SSM_W // out_tile):
        acc = None
        for ri in range(2):
            xs = jnp.concatenate(
                [x_scr[ri, j * k_chunks + c] for c in range(k_chunks)], axis=1)
            part = jnp.dot(xs, wc_ref[ri, j], preferred_element_type=F32)
            acc = part if acc is None else acc + part
        ys.append(acc)
    y = jnp.concatenate(ys, axis=1) + d_ref[...] * us_scr[0]
    y = jax.nn.gelu(y).astype(BF16)
    y = permute(pt_ref[...], y).astype(y_ref.dtype)
    y_ref[...] = y.reshape(nb, tc, SSM_W)


def _ssm_params(a_re, a_im, log_dt, b_re, b_im, c_re, c_im, d_skip):
    lr, li = a_re.astype(F32), a_im.astype(F32)
    dt = jnp.exp(log_dt.astype(F32))[:, None]
    mag = jnp.exp(lr * dt)
    ab_re, ab_im = mag * jnp.cos(li * dt), mag * jnp.sin(li * dt)
    den = lr * lr + li * li
    nr, ni = ab_re - 1.0, ab_im
    f_re = (nr * lr + ni * li) / den
    f_im = (ni * lr - nr * li) / den
    br, bi = b_re.astype(F32), b_im.astype(F32)
    bb_re = f_re[..., None] * br - f_im[..., None] * bi
    bb_im = f_re[..., None] * bi + f_im[..., None] * br
    tile_n = 2 * LANE
    n_bu = SSM_LANES // tile_n
    n_c = SSM_W // tile_n
    k_rows = tile_n // SSM_CH * SSM_STATE

    def bu_tiles(bb):
        rows = jnp.transpose(bb, (0, 2, 1)).reshape(SSM_W // LANE, LANE, SSM_STATE)
        k_slice = (np.arange(n_bu) * tile_n // SSM_STATE * SSM_CH) // LANE
        tiled = jnp.tile(rows[k_slice], (1, 1, tile_n // SSM_STATE))
        g_row = k_slice[:, None, None] * (LANE // SSM_CH) + np.arange(LANE)[None, :, None] // SSM_CH
        g_col = (np.arange(n_bu)[:, None, None] * tile_n + np.arange(tile_n)[None, None, :]) // SSM_STATE
        return jnp.where(g_row == g_col, tiled, 0.0)

    def c_tiles(c):
        rows = jnp.transpose(c, (0, 2, 1)).reshape(n_c, k_rows, SSM_CH)
        tiled = jnp.tile(rows, (1, 1, tile_n // SSM_CH))
        same_group = (np.arange(k_rows)[:, None] // SSM_STATE) == (np.arange(tile_n)[None, :] // SSM_CH)
        return jnp.where(same_group[None], tiled, 0.0)

    wbu = jnp.stack([bu_tiles(bb_re), bu_tiles(bb_im)]).astype(BF16)
    wc = jnp.stack([c_tiles(c_re.astype(F32)), c_tiles(-c_im.astype(F32))]).astype(BF16)
    n_chunks = SSM_LANES // LANE
    a = jnp.stack([ab_re.reshape(n_chunks, 1, LANE), ab_im.reshape(n_chunks, 1, LANE)])
    a = jnp.broadcast_to(a, (2, n_chunks, SUBLANE, LANE))
    d = d_skip.astype(F32).reshape(1, SSM_W)
    return wbu, wc, a, d


def _ssm_row_permutation(tc):
    dst = np.arange(SUBLANE * tc)
    src = (dst % SUBLANE) * tc + dst // SUBLANE
    p = src[:, None] == np.arange(SUBLANE * tc)[None, :]
    return jnp.asarray(p, dtype=BF16), jnp.asarray(p.T, dtype=BF16)


def _ssm(u, wbu, wc, a, d):
    B, S, _ = u.shape
    tc = SSM_TC
    n_chunks = SSM_LANES // LANE
    assert B % SUBLANE == 0 and S % tc == 0 and tc % (2 * SSM_UNROLL) == 0
    p_fwd, p_bwd = _ssm_row_permutation(tc)
    n_steps = S // tc
    return pl.pallas_call(
        _ssm_kernel,
        grid=(n_steps + 1,),
        in_specs=[
            pl.BlockSpec((B, tc, SSM_W), lambda i: (0, jnp.minimum(i, n_steps - 1), 0)),
            _const_spec(p_fwd.shape),
            _const_spec(p_bwd.shape),
            _const_spec(wbu.shape),
            _const_spec(wc.shape),
            _const_spec(a.shape),
            _const_spec(d.shape),
        ],
        out_specs=pl.BlockSpec((B, tc, SSM_W), lambda i: (0, jnp.maximum(i - 1, 0), 0)),
        out_shape=jax.ShapeDtypeStruct((B, S, SSM_W), BF16),
        scratch_shapes=[
            pltpu.VMEM((2, n_chunks, B * tc, LANE), F32),
            pltpu.VMEM((2, n_chunks, B * tc, LANE), BF16),
            pltpu.VMEM((2, n_chunks, B // SUBLANE, SUBLANE, LANE), F32),
            pltpu.VMEM((SSM_W // LANE, B * tc, LANE), BF16),
            pltpu.VMEM((2, B * tc, SSM_W), F32),
        ],
        compiler_params=pltpu.CompilerParams(
            dimension_semantics=("arbitrary",), vmem_limit_bytes=VMEM_LIMIT),
        name="ssm",
    )(u, p_fwd, p_bwd, wbu, wc, a, d)


def _mix_kernel(x_ref, attn_ref, y_ref, g_ref, wg_ref, wao_ref, wglu_ref, wout_ref, o_ref):
    x = x_ref[0]
    h = _rmsnorm(x, g_ref[...]).astype(BF16)
    gate = jax.nn.sigmoid(jnp.dot(h, wg_ref[...], preferred_element_type=F32))
    attn_d = jnp.dot(attn_ref[0], wao_ref[...], preferred_element_type=F32)
    z = jnp.dot(y_ref[0], wglu_ref[...], preferred_element_type=F32)
    ssm_out = z[:, :D_MODEL] * jax.nn.sigmoid(z[:, D_MODEL:])
    merged = gate[:, :D_MODEL] * attn_d + gate[:, D_MODEL:] * ssm_out
    o_ref[0] = x + jnp.dot(merged.astype(BF16), wout_ref[...], preferred_element_type=F32)


def _mix(x, attn, y, g, wg, wao, wglu, wout):
    B, S, D = x.shape
    tm = TM_MIX
    tok = lambda w: pl.BlockSpec((1, tm, w), lambda b, i: (b, i, 0))
    return pl.pallas_call(
        _mix_kernel,
        grid=(B, S // tm),
        in_specs=[tok(D), tok(ATTN_OUT_W), tok(SSM_W), _const_spec(g.shape), _const_spec(wg.shape),
                  _const_spec(wao.shape), _const_spec(wglu.shape), _const_spec(wout.shape)],
        out_specs=tok(D),
        out_shape=jax.ShapeDtypeStruct((B, S, D), F32),
        compiler_params=pltpu.CompilerParams(
            dimension_semantics=("arbitrary", "arbitrary"), vmem_limit_bytes=VMEM_LIMIT),
        name="mix",
    )(x, attn, y, g, wg, wao, wglu, wout)


def _ffn_kernel(x_ref, g2_ref, gf_ref, wgate_ref, wup_ref, wdown_ref, o_ref):
    x = x_ref[0]
    h = _rmsnorm(x, g2_ref[...]).astype(BF16)
    ff = None
    for c0, c1 in FFN_CHUNKS:
        gate = jnp.dot(h, wgate_ref[:, c0:c1], preferred_element_type=F32)
        up = jnp.dot(h, wup_ref[:, c0:c1], preferred_element_type=F32)
        act = (jax.nn.silu(gate) * up).astype(BF16)
        part = jnp.dot(act, wdown_ref[c0:c1, :], preferred_element_type=F32)
        ff = part if ff is None else ff + part
    o_ref[0] = _rmsnorm(x + ff, gf_ref[...])


def _ffn(x, g2, gf, wgate, wup, wdown):
    B, S, D = x.shape
    tm = TM_FFN
    tok = pl.BlockSpec((1, tm, D), lambda b, i: (b, i, 0))
    return pl.pallas_call(
        _ffn_kernel,
        grid=(B, S // tm),
        in_specs=[tok, _const_spec(g2.shape), _const_spec(gf.shape), _const_spec(wgate.shape),
                  _const_spec(wup.shape), _const_spec(wdown.shape)],
        out_specs=tok,
        out_shape=jax.ShapeDtypeStruct((B, S, D), F32),
        compiler_params=pltpu.CompilerParams(
            dimension_semantics=("arbitrary", "arbitrary"), vmem_limit_bytes=VMEM_LIMIT),
        name="ffn",
    )(x, g2, gf, wgate, wup, wdown)


def kernel(x, norm_mix_g, w_in, ssm_a_re, ssm_a_im, ssm_log_dt, ssm_b_re, ssm_b_im, ssm_c_re, ssm_c_im, ssm_d, w_glu, w_attn_out, w_out, norm_ffn_g, w_ffn_gate, w_ffn_up, w_ffn_down, norm_final_g):
    B, S, D = x.shape
    assert D == D_MODEL and norm_mix_g.shape[0] == 1, "single-layer block expected"
    layer = 0
    g_mix = norm_mix_g[layer].reshape(1, D).astype(F32)
    w_in_l = w_in[layer]
    w_qkvu = w_in_l[:, :QKVU_W].astype(BF16)
    w_gate = w_in_l[:, QKVU_W:].astype(BF16)

    qkv, u = _inproj(x, g_mix, w_qkvu, _rope_tables(S))
    attn = _attn(qkv)
    wbu, wc, a, d = _ssm_params(
        ssm_a_re[layer], ssm_a_im[layer], ssm_log_dt[layer], ssm_b_re[layer], ssm_b_im[layer],
        ssm_c_re[layer], ssm_c_im[layer], ssm_d[layer])
    y = _ssm(u, wbu, wc, a, d)
    x1 = _mix(x, attn, y, g_mix, w_gate, w_attn_out[layer].astype(BF16),
              w_glu[layer].astype(BF16), w_out[layer].astype(BF16))
    return _ffn(x1, norm_ffn_g[layer].reshape(1, D).astype(F32),
                norm_final_g.reshape(1, D).astype(F32), w_ffn_gate[layer].astype(BF16),
                w_ffn_up[layer].astype(BF16), w_ffn_down[layer].astype(BF16))
```

```python
import functools

import jax
import jax.numpy as jnp
import numpy as np
from jax import lax
from jax.experimental import pallas as pl
from jax.experimental.pallas import tpu as pltpu

F32 = jnp.float32
BF16 = jnp.bfloat16

D_MODEL = 1024
HEAD_DIM = 64
HEADS_PER_GROUP = 4
ATTN_GROUPS = ((128, 1), (512, 4), (2048, 16))
N_ATTN_HEADS = HEADS_PER_GROUP * len(ATTN_GROUPS)
ATTN_OUT_W = HEADS_PER_GROUP * HEAD_DIM
ROPE_DIM = HEAD_DIM // 4
ROPE_THETA = 500000.0
BLOCK = 128
SSM_CH = 16
SSM_GROUPS = 32
SSM_W = SSM_CH * SSM_GROUPS
SSM_STATE = 64
SSM_LANES = SSM_GROUPS * SSM_STATE
D_FF = 2816
QK_W = 2 * N_ATTN_HEADS * HEAD_DIM
QKV_W = 3 * N_ATTN_HEADS * HEAD_DIM
QKVU_W = QKV_W + SSM_W
RMS_EPS = 1e-6
NEG_INF = -1e30

LANE = 128
PERM_ROWS = 256
VMEM_LIMIT = 56 * 1024 * 1024

TS_INPROJ = 1024
TM_MIX = 1024
TM_FFN = 1024
FFN_CHUNKS = ((0, 1536), (1536, D_FF))
SSM_TC = 32
ATTN_UNROLL = 16
SSM_OUT_TILE = 2 * LANE
SSM_PAIR = 2
SSM_UNROLL = 8
SUBLANE = 8


def _rmsnorm(x, g):
    return x * lax.rsqrt(jnp.mean(x * x, axis=-1, keepdims=True) + RMS_EPS) * g


def _const_spec(shape):
    return pl.BlockSpec(shape, lambda *_: (0,) * len(shape), pipeline_mode=pl.Buffered(1))


def _inproj_kernel(x_ref, g_ref, w_ref, rc_ref, ra_ref, rb_ref, qkv_ref, u_ref):
    h = _rmsnorm(x_ref[0], g_ref[...]).astype(BF16)
    p = jnp.dot(h, w_ref[...], preferred_element_type=F32)
    rc, ra, rb = rc_ref[...], ra_ref[...], rb_ref[...]
    for c in range(QK_W // LANE):
        t = p[:, c * LANE:(c + 1) * LANE]
        half = ROPE_DIM // 2
        t = t * rc + pltpu.roll(t, LANE - half, 1) * ra + pltpu.roll(t, half, 1) * rb
        qkv_ref[0, :, c * LANE:(c + 1) * LANE] = t.astype(BF16)
    qkv_ref[0, :, QK_W:QKV_W] = p[:, QK_W:QKV_W].astype(BF16)
    u_ref[0] = p[:, QKV_W:QKVU_W]


def _rope_tables(S):
    half = ROPE_DIM // 2
    pos = np.arange(S, dtype=np.float64)
    inv = np.power(np.float64(ROPE_THETA), -np.arange(half, dtype=np.float64) * 2.0 / ROPE_DIM)
    ang = pos[:, None] * inv[None, :]
    cos, sin = np.cos(ang), np.sin(ang)
    ones = np.ones((S, HEAD_DIM - ROPE_DIM))
    zeros_h = np.zeros((S, half))
    zeros_r = np.zeros((S, HEAD_DIM - ROPE_DIM))
    per_head_c = np.concatenate([cos, cos, ones], axis=1)
    per_head_a = np.concatenate([-sin, zeros_h, zeros_r], axis=1)
    per_head_b = np.concatenate([zeros_h, sin, zeros_r], axis=1)
    rep = LANE // HEAD_DIM
    return tuple(jnp.asarray(np.tile(t, (1, rep)), dtype=F32)
                 for t in (per_head_c, per_head_a, per_head_b))


def _inproj(x, g, w, rope):
    B, S, D = x.shape
    ts = TS_INPROJ
    tab_spec = pl.BlockSpec((ts, LANE), lambda b, i: (i, 0))
    return pl.pallas_call(
        _inproj_kernel,
        grid=(B, S // ts),
        in_specs=[
            pl.BlockSpec((1, ts, D), lambda b, i: (b, i, 0)),
            _const_spec((1, D)),
            _const_spec((D, QKVU_W)),
            tab_spec, tab_spec, tab_spec,
        ],
        out_specs=[
            pl.BlockSpec((1, ts, QKV_W), lambda b, i: (b, i, 0)),
            pl.BlockSpec((1, ts, SSM_W), lambda b, i: (b, i, 0)),
        ],
        out_shape=[
            jax.ShapeDtypeStruct((B, S, QKV_W), BF16),
            jax.ShapeDtypeStruct((B, S, SSM_W), F32),
        ],
        compiler_params=pltpu.CompilerParams(
            dimension_semantics=("arbitrary", "arbitrary"), vmem_limit_bytes=VMEM_LIMIT),
        name="inproj",
    )(x, g, w, *rope)


def _split3(x):
    hi = x.astype(BF16)
    r1 = x - hi.astype(F32)
    mid = r1.astype(BF16)
    lo = (r1 - mid.astype(F32)).astype(BF16)
    return hi, mid, lo


def _head_combine(per_head, lane_lo):
    return jnp.concatenate(
        [jnp.where(lane_lo, per_head[0], per_head[1]), jnp.where(lane_lo, per_head[2], per_head[3])],
        axis=1)


def _attn_block(q, k, v, bias, hm, lane_lo):
    nh = HEADS_PER_GROUP
    qm = jnp.concatenate([q] * nh, axis=0) * hm
    s = lax.dot_general(qm, k, (((1,), (1,)), ((), ())), preferred_element_type=F32)
    o_h, lse_h = [], []
    for h in range(nh):
        sh = s[h * BLOCK:(h + 1) * BLOCK] + bias
        m = jnp.max(sh, axis=-1, keepdims=True)
        p = jnp.exp(sh - m)
        l = jnp.sum(p, axis=-1, keepdims=True)
        pv = jnp.dot(p.astype(BF16), v, preferred_element_type=F32)
        o_h.append(pv[:, (h // 2) * LANE:(h // 2 + 1) * LANE] * (1.0 / l))
        lse_h.append(jnp.broadcast_to(m + jnp.log(l), (BLOCK, LANE)))
    return _head_combine(o_h, lane_lo), _head_combine(lse_h, lane_lo)


def _attn_kernel(qkv_ref, hm_ref, pf4_ref, pf16_ref, o_ref, qs, ks, vs, o_scr, l_scr, bias_scr):
    S = qkv_ref.shape[1]
    n_chunks = ATTN_OUT_W // LANE
    perms = {4: pf4_ref, 16: pf16_ref}
    lane_lo = lax.broadcasted_iota(jnp.int32, (BLOCK, LANE), 1) < HEAD_DIM
    qi = lax.broadcasted_iota(jnp.int32, (BLOCK, 2 * BLOCK), 0)
    ki = lax.broadcasted_iota(jnp.int32, (BLOCK, 2 * BLOCK), 1)
    dist = qi + BLOCK - ki
    band = (dist >= 0) & (dist <= BLOCK)
    bias_scr[1] = jnp.where(band, 0.0, NEG_INF)
    bias_scr[0] = jnp.where(band & (ki >= BLOCK), 0.0, NEG_INF)
    hm = hm_ref[...]

    for gi, (window, d) in enumerate(ATTN_GROUPS):
        L = S // d
        nb = L // BLOCK
        cols = [part * N_ATTN_HEADS * HEAD_DIM + gi * ATTN_OUT_W for part in range(3)]
        plen = PERM_ROWS // d
        ppb = BLOCK // plen
        if d > 1:
            pf = perms[d][...]
            for scr, col in zip((qs, ks, vs), cols):
                for mblk in range(S // PERM_ROWS):
                    rs = slice(mblk * PERM_ROWS, (mblk + 1) * PERM_ROWS)
                    blk = qkv_ref[0, rs, col:col + ATTN_OUT_W]
                    scr[rs, :] = jnp.dot(pf, blk, preferred_element_type=F32).astype(BF16)

        def piece_rows(r, n, jj, d=d, plen=plen, ppb=ppb):
            if d == 1:
                return pl.ds(pl.multiple_of(n * BLOCK, BLOCK), BLOCK)
            return pl.ds(pl.multiple_of((n * ppb + jj) * PERM_ROWS + r * plen, plen), plen)

        def load(scr, col, r, n, d=d, ppb=ppb, piece_rows=piece_rows):
            if d == 1:
                return qkv_ref[0, piece_rows(r, n, 0), col:col + ATTN_OUT_W]
            return jnp.concatenate([scr[piece_rows(r, n, jj), :] for jj in range(ppb)], axis=0)

        def body(idx, carry, gi=gi, d=d, nb=nb, cols=cols, load=load, plen=plen, ppb=ppb,
                 piece_rows=piece_rows):
            r = idx // nb
            n = idx % nb
            q = load(qs, cols[0], r, n)
            k = load(ks, cols[1], r, n)
            v = load(vs, cols[2], r, n)
            if nb > 1:
                prev = jnp.maximum(n - 1, 0)
                k = jnp.concatenate([load(ks, cols[1], r, prev), k], axis=0)
                v = jnp.concatenate([load(vs, cols[2], r, prev), v], axis=0)
                bias = bias_scr[jnp.minimum(n, 1)]
            else:
                bias = bias_scr[1, :, BLOCK:]
            o, lse_b = _attn_block(q, k, v, bias, hm, lane_lo)
            start = r + d * BLOCK * n
            rows = pl.ds(pl.multiple_of(start, BLOCK), BLOCK) if d == 1 else pl.ds(start, BLOCK, stride=d)
            for c in range(n_chunks):
                o_scr[gi, c, rows, :] = o[:, c * LANE:(c + 1) * LANE]
                l_scr[gi, c, rows, :] = lse_b[:, c * LANE:(c + 1) * LANE]
            return carry

        lax.fori_loop(0, d * nb, body, 0, unroll=ATTN_UNROLL)

    n_groups = len(ATTN_GROUPS)

    def merge(i, carry):
        rs = pl.ds(pl.multiple_of(i * PERM_ROWS, PERM_ROWS), PERM_ROWS)
        for c in range(n_chunks):
            ls = [l_scr[g, c, rs, :] for g in range(n_groups)]
            m = functools.reduce(jnp.maximum, ls)
            es = [jnp.exp(l - m) for l in ls]
            inv_den = 1.0 / functools.reduce(lambda a, b: a + b, es)
            acc = None
            for g in range(n_groups):
                term = (es[g] * inv_den) * o_scr[g, c, rs, :]
                acc = term if acc is None else acc + term
            o_ref[0, rs, c * LANE:(c + 1) * LANE] = acc.astype(o_ref.dtype)
        return carry

    lax.fori_loop(0, S // PERM_ROWS, merge, 0)


def _residue_permutation(d):
    dst = np.arange(PERM_ROWS)
    plen = PERM_ROWS // d
    src = (dst % plen) * d + dst // plen
    return jnp.asarray(src[:, None] == np.arange(PERM_ROWS)[None, :], dtype=BF16)


def _attn(qkv):
    B, S, _ = qkv.shape
    n_groups = len(ATTN_GROUPS)
    for window, d in ATTN_GROUPS:
        assert window // d == BLOCK and S % (d * BLOCK) == 0
        assert d == 1 or (PERM_ROWS % d == 0 and BLOCK % (PERM_ROWS // d) == 0 and S % PERM_ROWS == 0)
    lane_head = np.arange(ATTN_OUT_W) // HEAD_DIM
    row_head = np.arange(HEADS_PER_GROUP * BLOCK) // BLOCK
    hm = jnp.asarray(np.where(row_head[:, None] == lane_head[None, :], HEAD_DIM ** -0.5, 0.0), dtype=BF16)
    n_chunks = ATTN_OUT_W // LANE
    perm_mats = [_residue_permutation(d) for _, d in ATTN_GROUPS if d > 1]
    return pl.pallas_call(
        _attn_kernel,
        grid=(B,),
        in_specs=[pl.BlockSpec((1, S, QKV_W), lambda b: (b, 0, 0)), _const_spec(hm.shape)]
        + [_const_spec(m.shape) for m in perm_mats],
        out_specs=pl.BlockSpec((1, S, ATTN_OUT_W), lambda b: (b, 0, 0)),
        out_shape=jax.ShapeDtypeStruct((B, S, ATTN_OUT_W), BF16),
        scratch_shapes=[
            pltpu.VMEM((S, ATTN_OUT_W), BF16),
            pltpu.VMEM((S, ATTN_OUT_W), BF16),
            pltpu.VMEM((S, ATTN_OUT_W), BF16),
            pltpu.VMEM((n_groups, n_chunks, S, LANE), F32),
            pltpu.VMEM((n_groups, n_chunks, S, LANE), F32),
            pltpu.VMEM((2, BLOCK, 2 * BLOCK), F32),
        ],
        compiler_params=pltpu.CompilerParams(
            dimension_semantics=("arbitrary",), vmem_limit_bytes=VMEM_LIMIT),
        name="attn",
    )(qkv, hm, *perm_mats)


def _ssm_kernel(u_ref, p_ref, pt_ref, wbu_ref, wc_ref, a_ref, d_ref, y_ref,
                bu_scr, x_scr, st_scr, ub_scr, us_scr):
    nb, tc, _ = u_ref.shape
    rows = nb * tc
    n_half = nb // SUBLANE
    hrows = SUBLANE * tc
    n_chunks = SSM_LANES // LANE
    tile_n = 2 * LANE
    assert SSM_PAIR * LANE == tile_n, "one scan pass covers the state lanes of one matmul tile"

    @pl.when(pl.program_id(0) == 0)
    def _():
        st_scr[...] = jnp.zeros_like(st_scr)
        bu_scr[...] = jnp.zeros_like(bu_scr)
        us_scr[1] = jnp.zeros(us_scr.shape[1:], F32)

    us_scr[0] = us_scr[1]

    def permute(pm, x):
        return jnp.concatenate(
            [jnp.dot(pm, x[h * hrows:(h + 1) * hrows], preferred_element_type=F32)
             for h in range(n_half)], axis=0)

    u_hi, u_mid, u_lo = _split3(u_ref[...].reshape(rows, SSM_W))
    p_fwd = p_ref[...]
    ut_hi = permute(p_fwd, u_hi)
    us_scr[1] = ut_hi + permute(p_fwd, u_mid) + permute(p_fwd, u_lo)
    ub = ut_hi.astype(BF16)
    for s in range(SSM_W // LANE):
        ub_scr[s] = ub[:, s * LANE:(s + 1) * LANE]

    for cp in range(n_chunks // SSM_PAIR):
        chains = [(cp * SSM_PAIR + j, h) for j in range(SSM_PAIR) for h in range(n_half)]
        a_re = {c: a_ref[0, c] for c, _ in chains}
        a_im = {c: a_ref[1, c] for c, _ in chains}
        carry = [st_scr[ri, c, h] for c, h in chains for ri in range(2)]
        pack = 4 // x_scr.dtype.itemsize
        for t0 in range(0, tc, pack):
            for j, (c, h) in enumerate(chains):
                xr, xi = carry[2 * j], carry[2 * j + 1]
                outs_r, outs_i = [], []
                for t in range(t0, t0 + pack):
                    row = pl.ds(h * hrows + SUBLANE * t, SUBLANE)
                    xr, xi = (a_re[c] * xr - a_im[c] * xi + bu_scr[0, c, row, :],
                              a_re[c] * xi + a_im[c] * xr + bu_scr[1, c, row, :])
                    outs_r.append(xr)
                    outs_i.append(xi)
                rows_p = pl.ds(h * hrows + SUBLANE * t0, SUBLANE * pack)
                x_scr[0, c, rows_p, :] = jnp.concatenate(outs_r, axis=0).astype(x_scr.dtype)
                x_scr[1, c, rows_p, :] = jnp.concatenate(outs_i, axis=0).astype(x_scr.dtype)
                carry[2 * j], carry[2 * j + 1] = xr, xi
        for j, (c, h) in enumerate(chains):
            st_scr[0, c, h] = carry[2 * j]
            st_scr[1, c, h] = carry[2 * j + 1]
        u_slice = (cp * tile_n // SSM_STATE * SSM_CH) // LANE
        for ri in range(2):
            bu = jnp.dot(ub_scr[u_slice], wbu_ref[ri, cp], preferred_element_type=F32)
            for c in range(SSM_PAIR):
                bu_scr[ri, cp * SSM_PAIR + c] = bu[:, c * LANE:(c + 1) * LANE]

    out_tile = SSM_OUT_TILE
    k_chunks = (out_tile // SSM_CH * SSM_STATE) // LANE
    ys = []
    for j in range(SSM_W // out_tile):
        acc = None
        for ri in range(2):
            xs = jnp.concatenate(
                [x_scr[ri, j * k_chunks + c] for c in range(k_chunks)], axis=1)
            part = jnp.dot(xs, wc_ref[ri, j], preferred_element_type=F32)
            acc = part if acc is None else acc + part
        ys.append(acc)
    y = jnp.concatenate(ys, axis=1) + d_ref[...] * us_scr[0]
    y = jax.nn.gelu(y).astype(BF16)
    y = permute(pt_ref[...], y).astype(y_ref.dtype)
    y_ref[...] = y.reshape(nb, tc, SSM_W)


def _ssm_params(a_re, a_im, log_dt, b_re, b_im, c_re, c_im, d_skip):
    lr, li = a_re.astype(F32), a_im.astype(F32)
    dt = jnp.exp(log_dt.astype(F32))[:, None]
    mag = jnp.exp(lr * dt)
    ab_re, ab_im = mag * jnp.cos(li * dt), mag * jnp.sin(li * dt)
    den = lr * lr + li * li
    nr, ni = ab_re - 1.0, ab_im
    f_re = (nr * lr + ni * li) / den
    f_im = (ni * lr - nr * li) / den
    br, bi = b_re.astype(F32), b_im.astype(F32)
    bb_re = f_re[..., None] * br - f_im[..., None] * bi
    bb_im = f_re[..., None] * bi + f_im[..., None] * br
    tile_n = 2 * LANE
    n_bu = SSM_LANES // tile_n
    n_c = SSM_W // SSM_OUT_TILE
    k_rows = SSM_OUT_TILE // SSM_CH * SSM_STATE

    def bu_tiles(bb):
        rows = jnp.transpose(bb, (0, 2, 1)).reshape(SSM_W // LANE, LANE, SSM_STATE)
        k_slice = (np.arange(n_bu) * tile_n // SSM_STATE * SSM_CH) // LANE
        tiled = jnp.tile(rows[k_slice], (1, 1, tile_n // SSM_STATE))
        g_row = k_slice[:, None, None] * (LANE // SSM_CH) + np.arange(LANE)[None, :, None] // SSM_CH
        g_col = (np.arange(n_bu)[:, None, None] * tile_n + np.arange(tile_n)[None, None, :]) // SSM_STATE
        return jnp.where(g_row == g_col, tiled, 0.0)

    def c_tiles(c):
        rows = jnp.transpose(c, (0, 2, 1)).reshape(n_c, k_rows, SSM_CH)
        tiled = jnp.tile(rows, (1, 1, SSM_OUT_TILE // SSM_CH))
        same_group = ((np.arange(k_rows)[:, None] // SSM_STATE)
                      == (np.arange(SSM_OUT_TILE)[None, :] // SSM_CH))
        return jnp.where(same_group[None], tiled, 0.0)

    wbu = jnp.stack([bu_tiles(bb_re), bu_tiles(bb_im)]).astype(BF16)
    wc = jnp.stack([c_tiles(c_re.astype(F32)), c_tiles(-c_im.astype(F32))]).astype(BF16)
    n_chunks = SSM_LANES // LANE
    a = jnp.stack([ab_re.reshape(n_chunks, 1, LANE), ab_im.reshape(n_chunks, 1, LANE)])
    a = jnp.broadcast_to(a, (2, n_chunks, SUBLANE, LANE))
    d = d_skip.astype(F32).reshape(1, SSM_W)
    return wbu, wc, a, d


def _ssm_row_permutation(tc):
    dst = np.arange(SUBLANE * tc)
    src = (dst % SUBLANE) * tc + dst // SUBLANE
    p = src[:, None] == np.arange(SUBLANE * tc)[None, :]
    return jnp.asarray(p, dtype=BF16), jnp.asarray(p.T, dtype=BF16)


def _ssm(u, wbu, wc, a, d):
    B, S, _ = u.shape
    tc = SSM_TC
    n_chunks = SSM_LANES // LANE
    assert B % SUBLANE == 0 and S % tc == 0 and tc % (2 * SSM_UNROLL) == 0
    p_fwd, p_bwd = _ssm_row_permutation(tc)
    n_steps = S // tc
    return pl.pallas_call(
        _ssm_kernel,
        grid=(n_steps + 1,),
        in_specs=[
            pl.BlockSpec((B, tc, SSM_W), lambda i: (0, jnp.minimum(i, n_steps - 1), 0)),
            _const_spec(p_fwd.shape),
            _const_spec(p_bwd.shape),
            _const_spec(wbu.shape),
            _const_spec(wc.shape),
            _const_spec(a.shape),
            _const_spec(d.shape),
        ],
        out_specs=pl.BlockSpec((B, tc, SSM_W), lambda i: (0, jnp.maximum(i - 1, 0), 0)),
        out_shape=jax.ShapeDtypeStruct((B, S, SSM_W), BF16),
        scratch_shapes=[
            pltpu.VMEM((2, n_chunks, B * tc, LANE), F32),
            pltpu.VMEM((2, n_chunks, B * tc, LANE), BF16),
            pltpu.VMEM((2, n_chunks, B // SUBLANE, SUBLANE, LANE), F32),
            pltpu.VMEM((SSM_W // LANE, B * tc, LANE), BF16),
            pltpu.VMEM((2, B * tc, SSM_W), F32),
        ],
        compiler_params=pltpu.CompilerParams(
            dimension_semantics=("arbitrary",), vmem_limit_bytes=VMEM_LIMIT),
        name="ssm",
    )(u, p_fwd, p_bwd, wbu, wc, a, d)


def _mix_kernel(x_ref, attn_ref, y_ref, g_ref, wg_ref, wao_ref, wglu_ref, wout_ref, o_ref):
    x = x_ref[0]
    h = _rmsnorm(x, g_ref[...]).astype(BF16)
    gate = jax.nn.sigmoid(jnp.dot(h, wg_ref[...], preferred_element_type=F32))
    attn_d = jnp.dot(attn_ref[0], wao_ref[...], preferred_element_type=F32)
    z = jnp.dot(y_ref[0], wglu_ref[...], preferred_element_type=F32)
    ssm_out = z[:, :D_MODEL] * jax.nn.sigmoid(z[:, D_MODEL:])
    merged = gate[:, :D_MODEL] * attn_d + gate[:, D_MODEL:] * ssm_out
    o_ref[0] = x + jnp.dot(merged.astype(BF16), wout_ref[...], preferred_element_type=F32)


def _mix(x, attn, y, g, wg, wao, wglu, wout):
    B, S, D = x.shape
    tm = TM_MIX
    tok = lambda w: pl.BlockSpec((1, tm, w), lambda b, i: (b, i, 0))
    return pl.pallas_call(
        _mix_kernel,
        grid=(B, S // tm),
        in_specs=[tok(D), tok(ATTN_OUT_W), tok(SSM_W), _const_spec(g.shape), _const_spec(wg.shape),
                  _const_spec(wao.shape), _const_spec(wglu.shape), _const_spec(wout.shape)],
        out_specs=tok(D),
        out_shape=jax.ShapeDtypeStruct((B, S, D), F32),
        compiler_params=pltpu.CompilerParams(
            dimension_semantics=("arbitrary", "arbitrary"), vmem_limit_bytes=VMEM_LIMIT),
        name="mix",
    )(x, attn, y, g, wg, wao, wglu, wout)


def _ffn_kernel(x_ref, g2_ref, gf_ref, wgate_ref, wup_ref, wdown_ref, o_ref):
    x = x_ref[0]
    h = _rmsnorm(x, g2_ref[...]).astype(BF16)
    ff = None
    for c0, c1 in FFN_CHUNKS:
        gate = jnp.dot(h, wgate_ref[:, c0:c1], preferred_element_type=F32)
        up = jnp.dot(h, wup_ref[:, c0:c1], preferred_element_type=F32)
        act = (jax.nn.silu(gate) * up).astype(BF16)
        part = jnp.dot(act, wdown_ref[c0:c1, :], preferred_element_type=F32)
        ff = part if ff is None else ff + part
    o_ref[0] = _rmsnorm(x + ff, gf_ref[...])


def _ffn(x, g2, gf, wgate, wup, wdown):
    B, S, D = x.shape
    tm = TM_FFN
    tok = pl.BlockSpec((1, tm, D), lambda b, i: (b, i, 0))
    return pl.pallas_call(
        _ffn_kernel,
        grid=(B, S // tm),
        in_specs=[tok, _const_spec(g2.shape), _const_spec(gf.shape), _const_spec(wgate.shape),
                  _const_spec(wup.shape), _const_spec(wdown.shape)],
        out_specs=tok,
        out_shape=jax.ShapeDtypeStruct((B, S, D), F32),
        compiler_params=pltpu.CompilerParams(
            dimension_semantics=("arbitrary", "arbitrary"), vmem_limit_bytes=VMEM_LIMIT),
        name="ffn",
    )(x, g2, gf, wgate, wup, wdown)


def kernel(x, norm_mix_g, w_in, ssm_a_re, ssm_a_im, ssm_log_dt, ssm_b_re, ssm_b_im, ssm_c_re, ssm_c_im, ssm_d, w_glu, w_attn_out, w_out, norm_ffn_g, w_ffn_gate, w_ffn_up, w_ffn_down, norm_final_g):
    B, S, D = x.shape
    assert D == D_MODEL and norm_mix_g.shape[0] == 1, "single-layer block expected"
    layer = 0
    g_mix = norm_mix_g[layer].reshape(1, D).astype(F32)
    w_in_l = w_in[layer]
    w_qkvu = w_in_l[:, :QKVU_W].astype(BF16)
    w_gate = w_in_l[:, QKVU_W:].astype(BF16)

    qkv, u = _inproj(x, g_mix, w_qkvu, _rope_tables(S))
    attn = _attn(qkv)
    wbu, wc, a, d = _ssm_params(
        ssm_a_re[layer], ssm_a_im[layer], ssm_log_dt[layer], ssm_b_re[layer], ssm_b_im[layer],
        ssm_c_re[layer], ssm_c_im[layer], ssm_d[layer])
    y = _ssm(u, wbu, wc, a, d)
    x1 = _mix(x, attn, y, g_mix, w_gate, w_attn_out[layer].astype(BF16),
              w_glu[layer].astype(BF16), w_out[layer].astype(BF16))
    return _ffn(x1, norm_ffn_g[layer].reshape(1, D).astype(F32),
                norm_final_g.reshape(1, D).astype(F32), w_ffn_gate[layer].astype(BF16),
                w_ffn_up[layer].astype(BF16), w_ffn_down[layer].astype(BF16))
```

```python
import functools

import jax
import jax.numpy as jnp
import numpy as np
from jax import lax
from jax.experimental import pallas as pl
from jax.experimental.pallas import tpu as pltpu

F32 = jnp.float32
BF16 = jnp.bfloat16

D_MODEL = 1024
HEAD_DIM = 64
HEADS_PER_GROUP = 4
ATTN_GROUPS = ((128, 1), (512, 4), (2048, 16))
N_ATTN_HEADS = HEADS_PER_GROUP * len(ATTN_GROUPS)
ATTN_OUT_W = HEADS_PER_GROUP * HEAD_DIM
ROPE_DIM = HEAD_DIM // 4
ROPE_THETA = 500000.0
BLOCK = 128
SSM_CH = 16
SSM_GROUPS = 32
SSM_W = SSM_CH * SSM_GROUPS
SSM_STATE = 64
SSM_LANES = SSM_GROUPS * SSM_STATE
D_FF = 2816
QK_W = 2 * N_ATTN_HEADS * HEAD_DIM
QKV_W = 3 * N_ATTN_HEADS * HEAD_DIM
QKVU_W = QKV_W + SSM_W
RMS_EPS = 1e-6
NEG_INF = -1e30

LANE = 128
PERM_ROWS = 256
VMEM_LIMIT = 56 * 1024 * 1024

TS_INPROJ = 1024
TM_MIX = 1024
TM_FFN = 1024
FFN_CHUNKS = ((0, 1536), (1536, D_FF))
SSM_TC = 32
ATTN_UNROLL = 16
SSM_OUT_TILE = 2 * LANE
SSM_PAIR = 2
SSM_UNROLL = 8
SUBLANE = 8


def _rmsnorm(x, g):
    return x * lax.rsqrt(jnp.mean(x * x, axis=-1, keepdims=True) + RMS_EPS) * g


def _const_spec(shape):
    return pl.BlockSpec(shape, lambda *_: (0,) * len(shape), pipeline_mode=pl.Buffered(1))


def _inproj_kernel(x_ref, g_ref, w_ref, rc_ref, ra_ref, rb_ref, qkv_ref, u_ref):
    h = _rmsnorm(x_ref[0], g_ref[...]).astype(BF16)
    p = jnp.dot(h, w_ref[...], preferred_element_type=F32)
    rc, ra, rb = rc_ref[...], ra_ref[...], rb_ref[...]
    for c in range(QK_W // LANE):
        t = p[:, c * LANE:(c + 1) * LANE]
        half = ROPE_DIM // 2
        t = t * rc + pltpu.roll(t, LANE - half, 1) * ra + pltpu.roll(t, half, 1) * rb
        qkv_ref[0, :, c * LANE:(c + 1) * LANE] = t.astype(BF16)
    qkv_ref[0, :, QK_W:QKV_W] = p[:, QK_W:QKV_W].astype(BF16)
    u_ref[0] = p[:, QKV_W:QKVU_W]


def _rope_tables(S):
    half = ROPE_DIM // 2
    pos = np.arange(S, dtype=np.float64)
    inv = np.power(np.float64(ROPE_THETA), -np.arange(half, dtype=np.float64) * 2.0 / ROPE_DIM)
    ang = pos[:, None] * inv[None, :]
    cos, sin = np.cos(ang), np.sin(ang)
    ones = np.ones((S, HEAD_DIM - ROPE_DIM))
    zeros_h = np.zeros((S, half))
    zeros_r = np.zeros((S, HEAD_DIM - ROPE_DIM))
    per_head_c = np.concatenate([cos, cos, ones], axis=1)
    per_head_a = np.concatenate([-sin, zeros_h, zeros_r], axis=1)
    per_head_b = np.concatenate([zeros_h, sin, zeros_r], axis=1)
    rep = LANE // HEAD_DIM
    return tuple(jnp.asarray(np.tile(t, (1, rep)), dtype=F32)
                 for t in (per_head_c, per_head_a, per_head_b))


def _inproj(x, g, w, rope):
    B, S, D = x.shape
    ts = TS_INPROJ
    tab_spec = pl.BlockSpec((ts, LANE), lambda b, i: (i, 0))
    return pl.pallas_call(
        _inproj_kernel,
        grid=(B, S // ts),
        in_specs=[
            pl.BlockSpec((1, ts, D), lambda b, i: (b, i, 0)),
            _const_spec((1, D)),
            _const_spec((D, QKVU_W)),
            tab_spec, tab_spec, tab_spec,
        ],
        out_specs=[
            pl.BlockSpec((1, ts, QKV_W), lambda b, i: (b, i, 0)),
            pl.BlockSpec((1, ts, SSM_W), lambda b, i: (b, i, 0)),
        ],
        out_shape=[
            jax.ShapeDtypeStruct((B, S, QKV_W), BF16),
            jax.ShapeDtypeStruct((B, S, SSM_W), F32),
        ],
        compiler_params=pltpu.CompilerParams(
            dimension_semantics=("arbitrary", "arbitrary"), vmem_limit_bytes=VMEM_LIMIT),
        name="inproj",
    )(x, g, w, *rope)


def _split3(x):
    hi = x.astype(BF16)
    r1 = x - hi.astype(F32)
    mid = r1.astype(BF16)
    lo = (r1 - mid.astype(F32)).astype(BF16)
    return hi, mid, lo


def _head_combine(per_head, lane_lo):
    return jnp.concatenate(
        [jnp.where(lane_lo, per_head[0], per_head[1]), jnp.where(lane_lo, per_head[2], per_head[3])],
        axis=1)


def _attn_block(q, k, v, bias, hm, lane_lo):
    nh = HEADS_PER_GROUP
    qm = jnp.concatenate([q] * nh, axis=0) * hm
    s = lax.dot_general(qm, k, (((1,), (1,)), ((), ())), preferred_element_type=F32)
    pv_h, m_h, l_h = [], [], []
    for h in range(nh):
        sh = s[h * BLOCK:(h + 1) * BLOCK] + bias
        m = jnp.max(sh, axis=-1, keepdims=True)
        p = jnp.exp(sh - m)
        l = jnp.sum(p, axis=-1, keepdims=True)
        pv = jnp.dot(p.astype(BF16), v, preferred_element_type=F32)
        pv_h.append(pv[:, (h // 2) * LANE:(h // 2 + 1) * LANE])
        m_h.append(jnp.broadcast_to(m, (BLOCK, LANE)))
        l_h.append(jnp.broadcast_to(l, (BLOCK, LANE)))
    l_b = _head_combine(l_h, lane_lo)
    o = _head_combine(pv_h, lane_lo) * (1.0 / l_b)
    return o, _head_combine(m_h, lane_lo) + jnp.log(l_b)


def _attn_kernel(qkv_ref, hm_ref, pf4_ref, pf16_ref, o_ref, qs, ks, vs, o_scr, l_scr, bias_scr):
    S = qkv_ref.shape[1]
    n_chunks = ATTN_OUT_W // LANE
    perms = {4: pf4_ref, 16: pf16_ref}
    lane_lo = lax.broadcasted_iota(jnp.int32, (BLOCK, LANE), 1) < HEAD_DIM
    qi = lax.broadcasted_iota(jnp.int32, (BLOCK, 2 * BLOCK), 0)
    ki = lax.broadcasted_iota(jnp.int32, (BLOCK, 2 * BLOCK), 1)
    dist = qi + BLOCK - ki
    band = (dist >= 0) & (dist <= BLOCK)
    bias_scr[1] = jnp.where(band, 0.0, NEG_INF)
    bias_scr[0] = jnp.where(band & (ki >= BLOCK), 0.0, NEG_INF)
    hm = hm_ref[...]

    for gi, (window, d) in enumerate(ATTN_GROUPS):
        L = S // d
        nb = L // BLOCK
        cols = [part * N_ATTN_HEADS * HEAD_DIM + gi * ATTN_OUT_W for part in range(3)]
        plen = PERM_ROWS // d
        ppb = BLOCK // plen
        if d > 1:
            pf = perms[d][...]
            for scr, col in zip((qs, ks, vs), cols):
                for mblk in range(S // PERM_ROWS):
                    rs = slice(mblk * PERM_ROWS, (mblk + 1) * PERM_ROWS)
                    blk = qkv_ref[0, rs, col:col + ATTN_OUT_W]
                    scr[rs, :] = jnp.dot(pf, blk, preferred_element_type=F32).astype(BF16)

        def piece_rows(r, n, jj, d=d, plen=plen, ppb=ppb):
            if d == 1:
                return pl.ds(pl.multiple_of(n * BLOCK, BLOCK), BLOCK)
            return pl.ds(pl.multiple_of((n * ppb + jj) * PERM_ROWS + r * plen, plen), plen)

        def load(scr, col, r, n, d=d, ppb=ppb, piece_rows=piece_rows):
            if d == 1:
                return qkv_ref[0, piece_rows(r, n, 0), col:col + ATTN_OUT_W]
            return jnp.concatenate([scr[piece_rows(r, n, jj), :] for jj in range(ppb)], axis=0)

        def body(idx, carry, gi=gi, d=d, nb=nb, cols=cols, load=load, plen=plen, ppb=ppb,
                 piece_rows=piece_rows):
            r = idx // nb
            n = idx % nb
            q = load(qs, cols[0], r, n)
            k = load(ks, cols[1], r, n)
            v = load(vs, cols[2], r, n)
            if nb > 1:
                prev = jnp.maximum(n - 1, 0)
                k = jnp.concatenate([load(ks, cols[1], r, prev), k], axis=0)
                v = jnp.concatenate([load(vs, cols[2], r, prev), v], axis=0)
                bias = bias_scr[jnp.minimum(n, 1)]
            else:
                bias = bias_scr[1, :, BLOCK:]
            o, lse_b = _attn_block(q, k, v, bias, hm, lane_lo)
            start = r + d * BLOCK * n
            rows = pl.ds(pl.multiple_of(start, BLOCK), BLOCK) if d == 1 else pl.ds(start, BLOCK, stride=d)
            for c in range(n_chunks):
                o_scr[gi, c, rows, :] = o[:, c * LANE:(c + 1) * LANE]
                l_scr[gi, c, rows, :] = lse_b[:, c * LANE:(c + 1) * LANE]
            return carry

        lax.fori_loop(0, d * nb, body, 0, unroll=ATTN_UNROLL)

    n_groups = len(ATTN_GROUPS)

    def merge(i, carry):
        rs = pl.ds(pl.multiple_of(i * PERM_ROWS, PERM_ROWS), PERM_ROWS)
        for c in range(n_chunks):
            ls = [l_scr[g, c, rs, :] for g in range(n_groups)]
            m = functools.reduce(jnp.maximum, ls)
            es = [jnp.exp(l - m) for l in ls]
            inv_den = 1.0 / functools.reduce(lambda a, b: a + b, es)
            acc = None
            for g in range(n_groups):
                term = (es[g] * inv_den) * o_scr[g, c, rs, :]
                acc = term if acc is None else acc + term
            o_ref[0, rs, c * LANE:(c + 1) * LANE] = acc.astype(o_ref.dtype)
        return carry

    lax.fori_loop(0, S // PERM_ROWS, merge, 0)


def _residue_permutation(d):
    dst = np.arange(PERM_ROWS)
    plen = PERM_ROWS // d
    src = (dst % plen) * d + dst // plen
    return jnp.asarray(src[:, None] == np.arange(PERM_ROWS)[None, :], dtype=BF16)


def _attn(qkv):
    B, S, _ = qkv.shape
    n_groups = len(ATTN_GROUPS)
    for window, d in ATTN_GROUPS:
        assert window // d == BLOCK and S % (d * BLOCK) == 0
        assert d == 1 or (PERM_ROWS % d == 0 and BLOCK % (PERM_ROWS // d) == 0 and S % PERM_ROWS == 0)
    lane_head = np.arange(ATTN_OUT_W) // HEAD_DIM
    row_head = np.arange(HEADS_PER_GROUP * BLOCK) // BLOCK
    hm = jnp.asarray(np.where(row_head[:, None] == lane_head[None, :], HEAD_DIM ** -0.5, 0.0), dtype=BF16)
    n_chunks = ATTN_OUT_W // LANE
    perm_mats = [_residue_permutation(d) for _, d in ATTN_GROUPS if d > 1]
    return pl.pallas_call(
        _attn_kernel,
        grid=(B,),
        in_specs=[pl.BlockSpec((1, S, QKV_W), lambda b: (b, 0, 0)), _const_spec(hm.shape)]
        + [_const_spec(m.shape) for m in perm_mats],
        out_specs=pl.BlockSpec((1, S, ATTN_OUT_W), lambda b: (b, 0, 0)),
        out_shape=jax.ShapeDtypeStruct((B, S, ATTN_OUT_W), BF16),
        scratch_shapes=[
            pltpu.VMEM((S, ATTN_OUT_W), BF16),
            pltpu.VMEM((S, ATTN_OUT_W), BF16),
            pltpu.VMEM((S, ATTN_OUT_W), BF16),
            pltpu.VMEM((n_groups, n_chunks, S, LANE), F32),
            pltpu.VMEM((n_groups, n_chunks, S, LANE), F32),
            pltpu.VMEM((2, BLOCK, 2 * BLOCK), F32),
        ],
        compiler_params=pltpu.CompilerParams(
            dimension_semantics=("arbitrary",), vmem_limit_bytes=VMEM_LIMIT),
        name="attn",
    )(qkv, hm, *perm_mats)


def _ssm_kernel(u_ref, p_ref, pt_ref, wbu_ref, wc_ref, a_ref, d_ref, y_ref,
                bu_scr, x_scr, st_scr, ub_scr, us_scr):
    nb, tc, _ = u_ref.shape
    rows = nb * tc
    n_half = nb // SUBLANE
    hrows = SUBLANE * tc
    n_chunks = SSM_LANES // LANE
    tile_n = 2 * LANE
    assert SSM_PAIR * LANE == tile_n, "one scan pass covers the state lanes of one matmul tile"

    @pl.when(pl.program_id(0) == 0)
    def _():
        st_scr[...] = jnp.zeros_like(st_scr)
        bu_scr[...] = jnp.zeros_like(bu_scr)
        x_scr[...] = jnp.zeros_like(x_scr)
        us_scr[1] = jnp.zeros(us_scr.shape[1:], F32)
        us_scr[2] = jnp.zeros(us_scr.shape[1:], F32)

    us_scr[0] = us_scr[1]
    us_scr[1] = us_scr[2]

    def permute(pm, x):
        return jnp.concatenate(
            [jnp.dot(pm, x[h * hrows:(h + 1) * hrows], preferred_element_type=F32)
             for h in range(n_half)], axis=0)

    u_hi, u_mid, u_lo = _split3(u_ref[...].reshape(rows, SSM_W))
    p_fwd = p_ref[...]
    ut_hi = permute(p_fwd, u_hi)
    us_scr[2] = ut_hi + permute(p_fwd, u_mid) + permute(p_fwd, u_lo)
    ub = ut_hi.astype(BF16)
    for s in range(SSM_W // LANE):
        ub_scr[s] = ub[:, s * LANE:(s + 1) * LANE]

    out_tile = SSM_OUT_TILE
    k_chunks = (out_tile // SSM_CH * SSM_STATE) // LANE
    cp_per_out = k_chunks // SSM_PAIR
    ys = []

    for cp in range(n_chunks // SSM_PAIR):
        if cp % cp_per_out == 0:
            j = cp // cp_per_out
            acc = None
            for ri in range(2):
                xs = jnp.concatenate(
                    [x_scr[ri, j * k_chunks + c] for c in range(k_chunks)], axis=1)
                part = jnp.dot(xs, wc_ref[ri, j], preferred_element_type=F32)
                acc = part if acc is None else acc + part
            ys.append(acc)
        chains = [(cp * SSM_PAIR + j, h) for j in range(SSM_PAIR) for h in range(n_half)]
        a_re = {c: a_ref[0, c] for c, _ in chains}
        a_im = {c: a_ref[1, c] for c, _ in chains}
        carry = [st_scr[ri, c, h] for c, h in chains for ri in range(2)]
        pack = 4 // x_scr.dtype.itemsize
        for t0 in range(0, tc, pack):
            for j, (c, h) in enumerate(chains):
                xr, xi = carry[2 * j], carry[2 * j + 1]
                outs_r, outs_i = [], []
                for t in range(t0, t0 + pack):
                    row = pl.ds(h * hrows + SUBLANE * t, SUBLANE)
                    xr, xi = (a_re[c] * xr - a_im[c] * xi + bu_scr[0, c, row, :],
                              a_re[c] * xi + a_im[c] * xr + bu_scr[1, c, row, :])
                    outs_r.append(xr)
                    outs_i.append(xi)
                rows_p = pl.ds(h * hrows + SUBLANE * t0, SUBLANE * pack)
                x_scr[0, c, rows_p, :] = jnp.concatenate(outs_r, axis=0).astype(x_scr.dtype)
                x_scr[1, c, rows_p, :] = jnp.concatenate(outs_i, axis=0).astype(x_scr.dtype)
                carry[2 * j], carry[2 * j + 1] = xr, xi
        for j, (c, h) in enumerate(chains):
            st_scr[0, c, h] = carry[2 * j]
            st_scr[1, c, h] = carry[2 * j + 1]
        u_slice = (cp * tile_n // SSM_STATE * SSM_CH) // LANE
        for ri in range(2):
            bu = jnp.dot(ub_scr[u_slice], wbu_ref[ri, cp], preferred_element_type=F32)
            for c in range(SSM_PAIR):
                bu_scr[ri, cp * SSM_PAIR + c] = bu[:, c * LANE:(c + 1) * LANE]

    y = jnp.concatenate(ys, axis=1) + d_ref[...] * us_scr[0]
    y = jax.nn.gelu(y).astype(BF16)
    y = permute(pt_ref[...], y).astype(y_ref.dtype)
    y_ref[...] = y.reshape(nb, tc, SSM_W)


def _ssm_params(a_re, a_im, log_dt, b_re, b_im, c_re, c_im, d_skip):
    lr, li = a_re.astype(F32), a_im.astype(F32)
    dt = jnp.exp(log_dt.astype(F32))[:, None]
    mag = jnp.exp(lr * dt)
    ab_re, ab_im = mag * jnp.cos(li * dt), mag * jnp.sin(li * dt)
    den = lr * lr + li * li
    nr, ni = ab_re - 1.0, ab_im
    f_re = (nr * lr + ni * li) / den
    f_im = (ni * lr - nr * li) / den
    br, bi = b_re.astype(F32), b_im.astype(F32)
    bb_re = f_re[..., None] * br - f_im[..., None] * bi
    bb_im = f_re[..., None] * bi + f_im[..., None] * br
    tile_n = 2 * LANE
    n_bu = SSM_LANES // tile_n
    n_c = SSM_W // SSM_OUT_TILE
    k_rows = SSM_OUT_TILE // SSM_CH * SSM_STATE

    def bu_tiles(bb):
        rows = jnp.transpose(bb, (0, 2, 1)).reshape(SSM_W // LANE, LANE, SSM_STATE)
        k_slice = (np.arange(n_bu) * tile_n // SSM_STATE * SSM_CH) // LANE
        tiled = jnp.tile(rows[k_slice], (1, 1, tile_n // SSM_STATE))
        g_row = k_slice[:, None, None] * (LANE // SSM_CH) + np.arange(LANE)[None, :, None] // SSM_CH
        g_col = (np.arange(n_bu)[:, None, None] * tile_n + np.arange(tile_n)[None, None, :]) // SSM_STATE
        return jnp.where(g_row == g_col, tiled, 0.0)

    def c_tiles(c):
        rows = jnp.transpose(c, (0, 2, 1)).reshape(n_c, k_rows, SSM_CH)
        tiled = jnp.tile(rows, (1, 1, SSM_OUT_TILE // SSM_CH))
        same_group = ((np.arange(k_rows)[:, None] // SSM_STATE)
                      == (np.arange(SSM_OUT_TILE)[None, :] // SSM_CH))
        return jnp.where(same_group[None], tiled, 0.0)

    wbu = jnp.stack([bu_tiles(bb_re), bu_tiles(bb_im)]).astype(BF16)
    wc = jnp.stack([c_tiles(c_re.astype(F32)), c_tiles(-c_im.astype(F32))]).astype(BF16)
    n_chunks = SSM_LANES // LANE
    a = jnp.stack([ab_re.reshape(n_chunks, 1, LANE), ab_im.reshape(n_chunks, 1, LANE)])
    a = jnp.broadcast_to(a, (2, n_chunks, SUBLANE, LANE))
    d = d_skip.astype(F32).reshape(1, SSM_W)
    return wbu, wc, a, d


def _ssm_row_permutation(tc):
    dst = np.arange(SUBLANE * tc)
    src = (dst % SUBLANE) * tc + dst // SUBLANE
    p = src[:, None] == np.arange(SUBLANE * tc)[None, :]
    return jnp.asarray(p, dtype=BF16), jnp.asarray(p.T, dtype=BF16)


def _ssm(u, wbu, wc, a, d):
    B, S, _ = u.shape
    tc = SSM_TC
    n_chunks = SSM_LANES // LANE
    assert B % SUBLANE == 0 and S % tc == 0 and tc % (2 * SSM_UNROLL) == 0
    p_fwd, p_bwd = _ssm_row_permutation(tc)
    n_steps = S // tc
    return pl.pallas_call(
        _ssm_kernel,
        grid=(n_steps + 2,),
        in_specs=[
            pl.BlockSpec((B, tc, SSM_W), lambda i: (0, jnp.minimum(i, n_steps - 1), 0)),
            _const_spec(p_fwd.shape),
            _const_spec(p_bwd.shape),
            _const_spec(wbu.shape),
            _const_spec(wc.shape),
            _const_spec(a.shape),
            _const_spec(d.shape),
        ],
        out_specs=pl.BlockSpec((B, tc, SSM_W), lambda i: (0, jnp.maximum(i - 2, 0), 0)),
        out_shape=jax.ShapeDtypeStruct((B, S, SSM_W), BF16),
        scratch_shapes=[
            pltpu.VMEM((2, n_chunks, B * tc, LANE), F32),
            pltpu.VMEM((2, n_chunks, B * tc, LANE), BF16),
            pltpu.VMEM((2, n_chunks, B // SUBLANE, SUBLANE, LANE), F32),
            pltpu.VMEM((SSM_W // LANE, B * tc, LANE), BF16),
            pltpu.VMEM((3, B * tc, SSM_W), F32),
        ],
        compiler_params=pltpu.CompilerParams(
            dimension_semantics=("arbitrary",), vmem_limit_bytes=VMEM_LIMIT),
        name="ssm",
    )(u, p_fwd, p_bwd, wbu, wc, a, d)


def _mix_kernel(x_ref, attn_ref, y_ref, g_ref, wg_ref, wao_ref, wglu_ref, wout_ref, o_ref):
    x = x_ref[0]
    h = _rmsnorm(x, g_ref[...]).astype(BF16)
    gate = jax.nn.sigmoid(jnp.dot(h, wg_ref[...], preferred_element_type=F32))
    attn_d = jnp.dot(attn_ref[0], wao_ref[...], preferred_element_type=F32)
    z = jnp.dot(y_ref[0], wglu_ref[...], preferred_element_type=F32)
    ssm_out = z[:, :D_MODEL] * jax.nn.sigmoid(z[:, D_MODEL:])
    merged = gate[:, :D_MODEL] * attn_d + gate[:, D_MODEL:] * ssm_out
    o_ref[0] = x + jnp.dot(merged.astype(BF16), wout_ref[...], preferred_element_type=F32)


def _mix(x, attn, y, g, wg, wao, wglu, wout):
    B, S, D = x.shape
    tm = TM_MIX
    tok = lambda w: pl.BlockSpec((1, tm, w), lambda b, i: (b, i, 0))
    return pl.pallas_call(
        _mix_kernel,
        grid=(B, S // tm),
        in_specs=[tok(D), tok(ATTN_OUT_W), tok(SSM_W), _const_spec(g.shape), _const_spec(wg.shape),
                  _const_spec(wao.shape), _const_spec(wglu.shape), _const_spec(wout.shape)],
        out_specs=tok(D),
        out_shape=jax.ShapeDtypeStruct((B, S, D), F32),
        compiler_params=pltpu.CompilerParams(
            dimension_semantics=("arbitrary", "arbitrary"), vmem_limit_bytes=VMEM_LIMIT),
        name="mix",
    )(x, attn, y, g, wg, wao, wglu, wout)


def _ffn_kernel(x_ref, g2_ref, gf_ref, wgate_ref, wup_ref, wdown_ref, o_ref):
    x = x_ref[0]
    h = _rmsnorm(x, g2_ref[...]).astype(BF16)
    ff = None
    for c0, c1 in FFN_CHUNKS:
        gate = jnp.dot(h, wgate_ref[:, c0:c1], preferred_element_type=F32)
        up = jnp.dot(h, wup_ref[:, c0:c1], preferred_element_type=F32)
        act = (jax.nn.silu(gate) * up).astype(BF16)
        part = jnp.dot(act, wdown_ref[c0:c1, :], preferred_element_type=F32)
        ff = part if ff is None else ff + part
    o_ref[0] = _rmsnorm(x + ff, gf_ref[...])


def _ffn(x, g2, gf, wgate, wup, wdown):
    B, S, D = x.shape
    tm = TM_FFN
    tok = pl.BlockSpec((1, tm, D), lambda b, i: (b, i, 0))
    return pl.pallas_call(
        _ffn_kernel,
        grid=(B, S // tm),
        in_specs=[tok, _const_spec(g2.shape), _const_spec(gf.shape), _const_spec(wgate.shape),
                  _const_spec(wup.shape), _const_spec(wdown.shape)],
        out_specs=tok,
        out_shape=jax.ShapeDtypeStruct((B, S, D), F32),
        compiler_params=pltpu.CompilerParams(
            dimension_semantics=("arbitrary", "arbitrary"), vmem_limit_bytes=VMEM_LIMIT),
        name="ffn",
    )(x, g2, gf, wgate, wup, wdown)


def kernel(x, norm_mix_g, w_in, ssm_a_re, ssm_a_im, ssm_log_dt, ssm_b_re, ssm_b_im, ssm_c_re, ssm_c_im, ssm_d, w_glu, w_attn_out, w_out, norm_ffn_g, w_ffn_gate, w_ffn_up, w_ffn_down, norm_final_g):
    B, S, D = x.shape
    assert D == D_MODEL and norm_mix_g.shape[0] == 1, "single-layer block expected"
    layer = 0
    g_mix = norm_mix_g[layer].reshape(1, D).astype(F32)
    w_in_l = w_in[layer]
    w_qkvu = w_in_l[:, :QKVU_W].astype(BF16)
    w_gate = w_in_l[:, QKVU_W:].astype(BF16)

    qkv, u = _inproj(x, g_mix, w_qkvu, _rope_tables(S))
    attn = _attn(qkv)
    wbu, wc, a, d = _ssm_params(
        ssm_a_re[layer], ssm_a_im[layer], ssm_log_dt[layer], ssm_b_re[layer], ssm_b_im[layer],
        ssm_c_re[layer], ssm_c_im[layer], ssm_d[layer])
    y = _ssm(u, wbu, wc, a, d)
    x1 = _mix(x, attn, y, g_mix, w_gate, w_attn_out[layer].astype(BF16),
              w_glu[layer].astype(BF16), w_out[layer].astype(BF16))
    return _ffn(x1, norm_ffn_g[layer].reshape(1, D).astype(F32),
                norm_final_g.reshape(1, D).astype(F32), w_ffn_gate[layer].astype(BF16),
                w_ffn_up[layer].astype(BF16), w_ffn_down[layer].astype(BF16))
```

```python
import functools

import jax
import jax.numpy as jnp
import numpy as np
from jax import lax
from jax.experimental import pallas as pl
from jax.experimental.pallas import tpu as pltpu

F32 = jnp.float32
BF16 = jnp.bfloat16

D_MODEL = 1024
HEAD_DIM = 64
HEADS_PER_GROUP = 4
ATTN_GROUPS = ((128, 1), (512, 4), (2048, 16))
N_ATTN_HEADS = HEADS_PER_GROUP * len(ATTN_GROUPS)
ATTN_OUT_W = HEADS_PER_GROUP * HEAD_DIM
ROPE_DIM = HEAD_DIM // 4
ROPE_THETA = 500000.0
BLOCK = 128
SSM_CH = 16
SSM_GROUPS = 32
SSM_W = SSM_CH * SSM_GROUPS
SSM_STATE = 64
SSM_LANES = SSM_GROUPS * SSM_STATE
D_FF = 2816
QK_W = 2 * N_ATTN_HEADS * HEAD_DIM
QKV_W = 3 * N_ATTN_HEADS * HEAD_DIM
QKVU_W = QKV_W + SSM_W
RMS_EPS = 1e-6
NEG_INF = -1e30

LANE = 128
PERM_ROWS = 256
VMEM_LIMIT = 56 * 1024 * 1024

TS_INPROJ = 1024
TM_MIX = 1024
TM_FFN = 1024
FFN_SPLIT = 4
MIX_SPLIT = 4
INPROJ_SPLIT = 4
SSM_TC = 32
ATTN_UNROLL = 16
SSM_OUT_TILE = 2 * LANE
SSM_PAIR = 2
SSM_UNROLL = 8
SUBLANE = 8


def _rmsnorm(x, g):
    return x * lax.rsqrt(jnp.mean(x * x, axis=-1, keepdims=True) + RMS_EPS) * g


def _const_spec(shape):
    return pl.BlockSpec(shape, lambda *_: (0,) * len(shape), pipeline_mode=pl.Buffered(1))


def _inproj_kernel(x_ref, g_ref, w_ref, rc_ref, ra_ref, rb_ref, qkv_ref, u_ref):
    ts = x_ref.shape[1]
    sub = ts // INPROJ_SPLIT
    half = ROPE_DIM // 2
    for r0 in range(0, ts, sub):
        rs = slice(r0, r0 + sub)
        h = _rmsnorm(x_ref[0, rs, :], g_ref[...]).astype(BF16)
        p = jnp.dot(h, w_ref[...], preferred_element_type=F32)
        rc, ra, rb = rc_ref[rs, :], ra_ref[rs, :], rb_ref[rs, :]
        for c in range(QK_W // LANE):
            t = p[:, c * LANE:(c + 1) * LANE]
            t = t * rc + pltpu.roll(t, LANE - half, 1) * ra + pltpu.roll(t, half, 1) * rb
            qkv_ref[0, rs, c * LANE:(c + 1) * LANE] = t.astype(BF16)
        qkv_ref[0, rs, QK_W:QKV_W] = p[:, QK_W:QKV_W].astype(BF16)
        u_ref[0, rs, :] = p[:, QKV_W:QKVU_W]


def _rope_tables(S):
    half = ROPE_DIM // 2
    pos = np.arange(S, dtype=np.float64)
    inv = np.power(np.float64(ROPE_THETA), -np.arange(half, dtype=np.float64) * 2.0 / ROPE_DIM)
    ang = pos[:, None] * inv[None, :]
    cos, sin = np.cos(ang), np.sin(ang)
    ones = np.ones((S, HEAD_DIM - ROPE_DIM))
    zeros_h = np.zeros((S, half))
    zeros_r = np.zeros((S, HEAD_DIM - ROPE_DIM))
    per_head_c = np.concatenate([cos, cos, ones], axis=1)
    per_head_a = np.concatenate([-sin, zeros_h, zeros_r], axis=1)
    per_head_b = np.concatenate([zeros_h, sin, zeros_r], axis=1)
    rep = LANE // HEAD_DIM
    return tuple(jnp.asarray(np.tile(t, (1, rep)), dtype=F32)
                 for t in (per_head_c, per_head_a, per_head_b))


def _inproj(x, g, w, rope):
    B, S, D = x.shape
    ts = TS_INPROJ
    tab_spec = pl.BlockSpec((ts, LANE), lambda b, i: (i, 0))
    return pl.pallas_call(
        _inproj_kernel,
        grid=(B, S // ts),
        in_specs=[
            pl.BlockSpec((1, ts, D), lambda b, i: (b, i, 0)),
            _const_spec((1, D)),
            _const_spec((D, QKVU_W)),
            tab_spec, tab_spec, tab_spec,
        ],
        out_specs=[
            pl.BlockSpec((1, ts, QKV_W), lambda b, i: (b, i, 0)),
            pl.BlockSpec((1, ts, SSM_W), lambda b, i: (b, i, 0)),
        ],
        out_shape=[
            jax.ShapeDtypeStruct((B, S, QKV_W), BF16),
            jax.ShapeDtypeStruct((B, S, SSM_W), F32),
        ],
        compiler_params=pltpu.CompilerParams(
            dimension_semantics=("arbitrary", "arbitrary"), vmem_limit_bytes=VMEM_LIMIT),
        name="inproj",
    )(x, g, w, *rope)


def _split3(x):
    hi = x.astype(BF16)
    r1 = x - hi.astype(F32)
    mid = r1.astype(BF16)
    lo = (r1 - mid.astype(F32)).astype(BF16)
    return hi, mid, lo


def _head_combine(per_head, lane_lo):
    return jnp.concatenate(
        [jnp.where(lane_lo, per_head[0], per_head[1]), jnp.where(lane_lo, per_head[2], per_head[3])],
        axis=1)


def _attn_block(q, k, v, bias, hm, lane_lo):
    nh = HEADS_PER_GROUP
    qm = jnp.concatenate([q] * nh, axis=0) * hm
    s = lax.dot_general(qm, k, (((1,), (1,)), ((), ())), preferred_element_type=F32)
    pv_h, m_h, l_h = [], [], []
    for h in range(nh):
        sh = s[h * BLOCK:(h + 1) * BLOCK] + bias
        m = jnp.max(sh, axis=-1, keepdims=True)
        p = jnp.exp(sh - m)
        l = jnp.sum(p, axis=-1, keepdims=True)
        pv = jnp.dot(p.astype(BF16), v, preferred_element_type=F32)
        pv_h.append(pv[:, (h // 2) * LANE:(h // 2 + 1) * LANE])
        m_h.append(jnp.broadcast_to(m, (BLOCK, LANE)))
        l_h.append(jnp.broadcast_to(l, (BLOCK, LANE)))
    l_b = _head_combine(l_h, lane_lo)
    o = _head_combine(pv_h, lane_lo) * (1.0 / l_b)
    return o, _head_combine(m_h, lane_lo) + jnp.log(l_b)


def _attn_kernel(qkv_ref, hm_ref, pf4_ref, pf16_ref, o_ref, qs, ks, vs, o_scr, l_scr, bias_scr):
    S = qkv_ref.shape[1]
    n_chunks = ATTN_OUT_W // LANE
    perms = {4: pf4_ref, 16: pf16_ref}
    lane_lo = lax.broadcasted_iota(jnp.int32, (BLOCK, LANE), 1) < HEAD_DIM
    qi = lax.broadcasted_iota(jnp.int32, (BLOCK, 2 * BLOCK), 0)
    ki = lax.broadcasted_iota(jnp.int32, (BLOCK, 2 * BLOCK), 1)
    dist = qi + BLOCK - ki
    band = (dist >= 0) & (dist <= BLOCK)
    bias_scr[1] = jnp.where(band, 0.0, NEG_INF)
    bias_scr[0] = jnp.where(band & (ki >= BLOCK), 0.0, NEG_INF)
    hm = hm_ref[...]

    for gi, (window, d) in enumerate(ATTN_GROUPS):
        L = S // d
        nb = L // BLOCK
        cols = [part * N_ATTN_HEADS * HEAD_DIM + gi * ATTN_OUT_W for part in range(3)]
        plen = PERM_ROWS // d
        ppb = BLOCK // plen
        if d > 1:
            pf = perms[d][...]
            for scr, col in zip((qs, ks, vs), cols):
                for mblk in range(S // PERM_ROWS):
                    rs = slice(mblk * PERM_ROWS, (mblk + 1) * PERM_ROWS)
                    blk = qkv_ref[0, rs, col:col + ATTN_OUT_W]
                    scr[rs, :] = jnp.dot(pf, blk, preferred_element_type=F32).astype(BF16)

        def piece_rows(r, n, jj, d=d, plen=plen, ppb=ppb):
            if d == 1:
                return pl.ds(pl.multiple_of(n * BLOCK, BLOCK), BLOCK)
            return pl.ds(pl.multiple_of((n * ppb + jj) * PERM_ROWS + r * plen, plen), plen)

        def load(scr, col, r, n, d=d, ppb=ppb, piece_rows=piece_rows):
            if d == 1:
                return qkv_ref[0, piece_rows(r, n, 0), col:col + ATTN_OUT_W]
            return jnp.concatenate([scr[piece_rows(r, n, jj), :] for jj in range(ppb)], axis=0)

        def body(idx, carry, gi=gi, d=d, nb=nb, cols=cols, load=load, plen=plen, ppb=ppb,
                 piece_rows=piece_rows):
            r = idx // nb
            n = idx % nb
            q = load(qs, cols[0], r, n)
            k = load(ks, cols[1], r, n)
            v = load(vs, cols[2], r, n)
            if nb > 1:
                prev = jnp.maximum(n - 1, 0)
                k = jnp.concatenate([load(ks, cols[1], r, prev), k], axis=0)
                v = jnp.concatenate([load(vs, cols[2], r, prev), v], axis=0)
                bias = bias_scr[jnp.minimum(n, 1)]
            else:
                bias = bias_scr[1, :, BLOCK:]
            o, lse_b = _attn_block(q, k, v, bias, hm, lane_lo)
            start = r + d * BLOCK * n
            rows = pl.ds(pl.multiple_of(start, BLOCK), BLOCK) if d == 1 else pl.ds(start, BLOCK, stride=d)
            for c in range(n_chunks):
                o_scr[gi, c, rows, :] = o[:, c * LANE:(c + 1) * LANE]
                l_scr[gi, c, rows, :] = lse_b[:, c * LANE:(c + 1) * LANE]
            return carry

        lax.fori_loop(0, d * nb, body, 0, unroll=ATTN_UNROLL)

    n_groups = len(ATTN_GROUPS)

    def merge(i, carry):
        rs = pl.ds(pl.multiple_of(i * PERM_ROWS, PERM_ROWS), PERM_ROWS)
        for c in range(n_chunks):
            ls = [l_scr[g, c, rs, :] for g in range(n_groups)]
            m = functools.reduce(jnp.maximum, ls)
            es = [jnp.exp(l - m) for l in ls]
            inv_den = 1.0 / functools.reduce(lambda a, b: a + b, es)
            acc = None
            for g in range(n_groups):
                term = (es[g] * inv_den) * o_scr[g, c, rs, :]
                acc = term if acc is None else acc + term
            o_ref[0, rs, c * LANE:(c + 1) * LANE] = acc.astype(o_ref.dtype)
        return carry

    lax.fori_loop(0, S // PERM_ROWS, merge, 0)


def _residue_permutation(d):
    dst = np.arange(PERM_ROWS)
    plen = PERM_ROWS // d
    src = (dst % plen) * d + dst // plen
    return jnp.asarray(src[:, None] == np.arange(PERM_ROWS)[None, :], dtype=BF16)


def _attn(qkv):
    B, S, _ = qkv.shape
    n_groups = len(ATTN_GROUPS)
    for window, d in ATTN_GROUPS:
        assert window // d == BLOCK and S % (d * BLOCK) == 0
        assert d == 1 or (PERM_ROWS % d == 0 and BLOCK % (PERM_ROWS // d) == 0 and S % PERM_ROWS == 0)
    lane_head = np.arange(ATTN_OUT_W) // HEAD_DIM
    row_head = np.arange(HEADS_PER_GROUP * BLOCK) // BLOCK
    hm = jnp.asarray(np.where(row_head[:, None] == lane_head[None, :], HEAD_DIM ** -0.5, 0.0), dtype=BF16)
    n_chunks = ATTN_OUT_W // LANE
    perm_mats = [_residue_permutation(d) for _, d in ATTN_GROUPS if d > 1]
    return pl.pallas_call(
        _attn_kernel,
        grid=(B,),
        in_specs=[pl.BlockSpec((1, S, QKV_W), lambda b: (b, 0, 0)), _const_spec(hm.shape)]
        + [_const_spec(m.shape) for m in perm_mats],
        out_specs=pl.BlockSpec((1, S, ATTN_OUT_W), lambda b: (b, 0, 0)),
        out_shape=jax.ShapeDtypeStruct((B, S, ATTN_OUT_W), BF16),
        scratch_shapes=[
            pltpu.VMEM((S, ATTN_OUT_W), BF16),
            pltpu.VMEM((S, ATTN_OUT_W), BF16),
            pltpu.VMEM((S, ATTN_OUT_W), BF16),
            pltpu.VMEM((n_groups, n_chunks, S, LANE), F32),
            pltpu.VMEM((n_groups, n_chunks, S, LANE), F32),
            pltpu.VMEM((2, BLOCK, 2 * BLOCK), F32),
        ],
        compiler_params=pltpu.CompilerParams(
            dimension_semantics=("arbitrary",), vmem_limit_bytes=VMEM_LIMIT),
        name="attn",
    )(qkv, hm, *perm_mats)


def _ssm_kernel(u_ref, p_ref, pt_ref, wbu_ref, wc_ref, a_ref, d_ref, y_ref,
                bu_scr, x_scr, st_scr, ub_scr, us_scr):
    nb, tc, _ = u_ref.shape
    rows = nb * tc
    n_half = nb // SUBLANE
    hrows = SUBLANE * tc
    n_chunks = SSM_LANES // LANE
    tile_n = 2 * LANE
    assert SSM_PAIR * LANE == tile_n, "one scan pass covers the state lanes of one matmul tile"

    @pl.when(pl.program_id(0) == 0)
    def _():
        st_scr[...] = jnp.zeros_like(st_scr)
        bu_scr[...] = jnp.zeros_like(bu_scr)
        x_scr[...] = jnp.zeros_like(x_scr)
        us_scr[1] = jnp.zeros(us_scr.shape[1:], F32)
        us_scr[2] = jnp.zeros(us_scr.shape[1:], F32)

    us_scr[0] = us_scr[1]
    us_scr[1] = us_scr[2]

    def permute(pm, x):
        return jnp.concatenate(
            [jnp.dot(pm, x[h * hrows:(h + 1) * hrows], preferred_element_type=F32)
             for h in range(n_half)], axis=0)

    u_hi, u_mid, u_lo = _split3(u_ref[...].reshape(rows, SSM_W))
    p_fwd = p_ref[...]
    ut_hi = permute(p_fwd, u_hi)
    us_scr[2] = ut_hi + permute(p_fwd, u_mid) + permute(p_fwd, u_lo)
    ub = ut_hi.astype(BF16)
    for s in range(SSM_W // LANE):
        ub_scr[s] = ub[:, s * LANE:(s + 1) * LANE]

    out_tile = SSM_OUT_TILE
    k_chunks = (out_tile // SSM_CH * SSM_STATE) // LANE
    cp_per_out = k_chunks // SSM_PAIR
    ys = []

    for cp in range(n_chunks // SSM_PAIR):
        if cp % cp_per_out == 0:
            j = cp // cp_per_out
            acc = None
            for ri in range(2):
                xs = jnp.concatenate(
                    [x_scr[ri, j * k_chunks + c] for c in range(k_chunks)], axis=1)
                part = jnp.dot(xs, wc_ref[ri, j], preferred_element_type=F32)
                acc = part if acc is None else acc + part
            ys.append(acc)
        chains = [(cp * SSM_PAIR + j, h) for j in range(SSM_PAIR) for h in range(n_half)]
        a_re = {c: a_ref[0, c] for c, _ in chains}
        a_im = {c: a_ref[1, c] for c, _ in chains}
        carry = [st_scr[ri, c, h] for c, h in chains for ri in range(2)]
        pack = 4 // x_scr.dtype.itemsize
        for t0 in range(0, tc, pack):
            for j, (c, h) in enumerate(chains):
                xr, xi = carry[2 * j], carry[2 * j + 1]
                outs_r, outs_i = [], []
                for t in range(t0, t0 + pack):
                    row = pl.ds(h * hrows + SUBLANE * t, SUBLANE)
                    xr, xi = (a_re[c] * xr - a_im[c] * xi + bu_scr[0, c, row, :],
                              a_re[c] * xi + a_im[c] * xr + bu_scr[1, c, row, :])
                    outs_r.append(xr)
                    outs_i.append(xi)
                rows_p = pl.ds(h * hrows + SUBLANE * t0, SUBLANE * pack)
                x_scr[0, c, rows_p, :] = jnp.concatenate(outs_r, axis=0).astype(x_scr.dtype)
                x_scr[1, c, rows_p, :] = jnp.concatenate(outs_i, axis=0).astype(x_scr.dtype)
                carry[2 * j], carry[2 * j + 1] = xr, xi
        for j, (c, h) in enumerate(chains):
            st_scr[0, c, h] = carry[2 * j]
            st_scr[1, c, h] = carry[2 * j + 1]
        u_slice = (cp * tile_n // SSM_STATE * SSM_CH) // LANE
        for ri in range(2):
            bu = jnp.dot(ub_scr[u_slice], wbu_ref[ri, cp], preferred_element_type=F32)
            for c in range(SSM_PAIR):
                bu_scr[ri, cp * SSM_PAIR + c] = bu[:, c * LANE:(c + 1) * LANE]

    y = jnp.concatenate(ys, axis=1) + d_ref[...] * us_scr[0]
    y = jax.nn.gelu(y).astype(BF16)
    y = permute(pt_ref[...], y).astype(y_ref.dtype)
    y_ref[...] = y.reshape(nb, tc, SSM_W)


def _ssm_params(a_re, a_im, log_dt, b_re, b_im, c_re, c_im, d_skip):
    lr, li = a_re.astype(F32), a_im.astype(F32)
    dt = jnp.exp(log_dt.astype(F32))[:, None]
    mag = jnp.exp(lr * dt)
    ab_re, ab_im = mag * jnp.cos(li * dt), mag * jnp.sin(li * dt)
    den = lr * lr + li * li
    nr, ni = ab_re - 1.0, ab_im
    f_re = (nr * lr + ni * li) / den
    f_im = (ni * lr - nr * li) / den
    br, bi = b_re.astype(F32), b_im.astype(F32)
    bb_re = f_re[..., None] * br - f_im[..., None] * bi
    bb_im = f_re[..., None] * bi + f_im[..., None] * br
    tile_n = 2 * LANE
    n_bu = SSM_LANES // tile_n
    n_c = SSM_W // SSM_OUT_TILE
    k_rows = SSM_OUT_TILE // SSM_CH * SSM_STATE

    def bu_tiles(bb):
        rows = jnp.transpose(bb, (0, 2, 1)).reshape(SSM_W // LANE, LANE, SSM_STATE)
        k_slice = (np.arange(n_bu) * tile_n // SSM_STATE * SSM_CH) // LANE
        tiled = jnp.tile(rows[k_slice], (1, 1, tile_n // SSM_STATE))
        g_row = k_slice[:, None, None] * (LANE // SSM_CH) + np.arange(LANE)[None, :, None] // SSM_CH
        g_col = (np.arange(n_bu)[:, None, None] * tile_n + np.arange(tile_n)[None, None, :]) // SSM_STATE
        return jnp.where(g_row == g_col, tiled, 0.0)

    def c_tiles(c):
        rows = jnp.transpose(c, (0, 2, 1)).reshape(n_c, k_rows, SSM_CH)
        tiled = jnp.tile(rows, (1, 1, SSM_OUT_TILE // SSM_CH))
        same_group = ((np.arange(k_rows)[:, None] // SSM_STATE)
                      == (np.arange(SSM_OUT_TILE)[None, :] // SSM_CH))
        return jnp.where(same_group[None], tiled, 0.0)

    wbu = jnp.stack([bu_tiles(bb_re), bu_tiles(bb_im)]).astype(BF16)
    wc = jnp.stack([c_tiles(c_re.astype(F32)), c_tiles(-c_im.astype(F32))]).astype(BF16)
    n_chunks = SSM_LANES // LANE
    a = jnp.stack([ab_re.reshape(n_chunks, 1, LANE), ab_im.reshape(n_chunks, 1, LANE)])
    a = jnp.broadcast_to(a, (2, n_chunks, SUBLANE, LANE))
    d = d_skip.astype(F32).reshape(1, SSM_W)
    return wbu, wc, a, d


def _ssm_row_permutation(tc):
    dst = np.arange(SUBLANE * tc)
    src = (dst % SUBLANE) * tc + dst // SUBLANE
    p = src[:, None] == np.arange(SUBLANE * tc)[None, :]
    return jnp.asarray(p, dtype=BF16), jnp.asarray(p.T, dtype=BF16)


def _ssm(u, wbu, wc, a, d):
    B, S, _ = u.shape
    tc = SSM_TC
    n_chunks = SSM_LANES // LANE
    assert B % SUBLANE == 0 and S % tc == 0 and tc % (2 * SSM_UNROLL) == 0
    p_fwd, p_bwd = _ssm_row_permutation(tc)
    n_steps = S // tc
    return pl.pallas_call(
        _ssm_kernel,
        grid=(n_steps + 2,),
        in_specs=[
            pl.BlockSpec((B, tc, SSM_W), lambda i: (0, jnp.minimum(i, n_steps - 1), 0)),
            _const_spec(p_fwd.shape),
            _const_spec(p_bwd.shape),
            _const_spec(wbu.shape),
            _const_spec(wc.shape),
            _const_spec(a.shape),
            _const_spec(d.shape),
        ],
        out_specs=pl.BlockSpec((B, tc, SSM_W), lambda i: (0, jnp.maximum(i - 2, 0), 0)),
        out_shape=jax.ShapeDtypeStruct((B, S, SSM_W), BF16),
        scratch_shapes=[
            pltpu.VMEM((2, n_chunks, B * tc, LANE), F32),
            pltpu.VMEM((2, n_chunks, B * tc, LANE), BF16),
            pltpu.VMEM((2, n_chunks, B // SUBLANE, SUBLANE, LANE), F32),
            pltpu.VMEM((SSM_W // LANE, B * tc, LANE), BF16),
            pltpu.VMEM((3, B * tc, SSM_W), F32),
        ],
        compiler_params=pltpu.CompilerParams(
            dimension_semantics=("arbitrary",), vmem_limit_bytes=VMEM_LIMIT),
        name="ssm",
    )(u, p_fwd, p_bwd, wbu, wc, a, d)


def _mix_kernel(x_ref, attn_ref, y_ref, g_ref, wg_ref, wao_ref, wglu_ref, wout_ref, o_ref):
    tm = x_ref.shape[1]
    sub = tm // MIX_SPLIT
    for r0 in range(0, tm, sub):
        rs = slice(r0, r0 + sub)
        x = x_ref[0, rs, :]
        h = _rmsnorm(x, g_ref[...]).astype(BF16)
        gate = jax.nn.sigmoid(jnp.dot(h, wg_ref[...], preferred_element_type=F32))
        attn_d = jnp.dot(attn_ref[0, rs, :], wao_ref[...], preferred_element_type=F32)
        z = jnp.dot(y_ref[0, rs, :], wglu_ref[...], preferred_element_type=F32)
        ssm_out = z[:, :D_MODEL] * jax.nn.sigmoid(z[:, D_MODEL:])
        merged = gate[:, :D_MODEL] * attn_d + gate[:, D_MODEL:] * ssm_out
        o_ref[0, rs, :] = x + jnp.dot(merged.astype(BF16), wout_ref[...], preferred_element_type=F32)


def _mix(x, attn, y, g, wg, wao, wglu, wout):
    B, S, D = x.shape
    tm = TM_MIX
    tok = lambda w: pl.BlockSpec((1, tm, w), lambda b, i: (b, i, 0))
    return pl.pallas_call(
        _mix_kernel,
        grid=(B, S // tm),
        in_specs=[tok(D), tok(ATTN_OUT_W), tok(SSM_W), _const_spec(g.shape), _const_spec(wg.shape),
                  _const_spec(wao.shape), _const_spec(wglu.shape), _const_spec(wout.shape)],
        out_specs=tok(D),
        out_shape=jax.ShapeDtypeStruct((B, S, D), F32),
        compiler_params=pltpu.CompilerParams(
            dimension_semantics=("arbitrary", "arbitrary"), vmem_limit_bytes=VMEM_LIMIT),
        name="mix",
    )(x, attn, y, g, wg, wao, wglu, wout)


def _ffn_kernel(x_ref, g2_ref, gf_ref, wgate_ref, wup_ref, wdown_ref, o_ref):
    tm = x_ref.shape[1]
    sub = tm // FFN_SPLIT
    for r0 in range(0, tm, sub):
        rs = slice(r0, r0 + sub)
        x = x_ref[0, rs, :]
        h = _rmsnorm(x, g2_ref[...]).astype(BF16)
        gate = jnp.dot(h, wgate_ref[...], preferred_element_type=F32)
        up = jnp.dot(h, wup_ref[...], preferred_element_type=F32)
        act = (jax.nn.silu(gate) * up).astype(BF16)
        ff = jnp.dot(act, wdown_ref[...], preferred_element_type=F32)
        o_ref[0, rs, :] = _rmsnorm(x + ff, gf_ref[...])


def _ffn(x, g2, gf, wgate, wup, wdown):
    B, S, D = x.shape
    tm = TM_FFN
    tok = pl.BlockSpec((1, tm, D), lambda b, i: (b, i, 0))
    return pl.pallas_call(
        _ffn_kernel,
        grid=(B, S // tm),
        in_specs=[tok, _const_spec(g2.shape), _const_spec(gf.shape), _const_spec(wgate.shape),
                  _const_spec(wup.shape), _const_spec(wdown.shape)],
        out_specs=tok,
        out_shape=jax.ShapeDtypeStruct((B, S, D), F32),
        compiler_params=pltpu.CompilerParams(
            dimension_semantics=("arbitrary", "arbitrary"), vmem_limit_bytes=VMEM_LIMIT),
        name="ffn",
    )(x, g2, gf, wgate, wup, wdown)


def kernel(x, norm_mix_g, w_in, ssm_a_re, ssm_a_im, ssm_log_dt, ssm_b_re, ssm_b_im, ssm_c_re, ssm_c_im, ssm_d, w_glu, w_attn_out, w_out, norm_ffn_g, w_ffn_gate, w_ffn_up, w_ffn_down, norm_final_g):
    B, S, D = x.shape
    assert D == D_MODEL and norm_mix_g.shape[0] == 1, "single-layer block expected"
    layer = 0
    g_mix = norm_mix_g[layer].reshape(1, D).astype(F32)
    w_in_l = w_in[layer]
    w_qkvu = w_in_l[:, :QKVU_W].astype(BF16)
    w_gate = w_in_l[:, QKVU_W:].astype(BF16)

    qkv, u = _inproj(x, g_mix, w_qkvu, _rope_tables(S))
    attn = _attn(qkv)
    wbu, wc, a, d = _ssm_params(
        ssm_a_re[layer], ssm_a_im[layer], ssm_log_dt[layer], ssm_b_re[layer], ssm_b_im[layer],
        ssm_c_re[layer], ssm_c_im[layer], ssm_d[layer])
    y = _ssm(u, wbu, wc, a, d)
    x1 = _mix(x, attn, y, g_mix, w_gate, w_attn_out[layer].astype(BF16),
              w_glu[layer].astype(BF16), w_out[layer].astype(BF16))
    return _ffn(x1, norm_ffn_g[layer].reshape(1, D).astype(F32),
                norm_final_g.reshape(1, D).astype(F32), w_ffn_gate[layer].astype(BF16),
                w_ffn_up[layer].astype(BF16), w_ffn_down[layer].astype(BF16))
```

```python
import functools

import jax
import jax.numpy as jnp
import numpy as np
from jax import lax
from jax.experimental import pallas as pl
from jax.experimental.pallas import tpu as pltpu

F32 = jnp.float32
BF16 = jnp.bfloat16

D_MODEL = 1024
HEAD_DIM = 64
HEADS_PER_GROUP = 4
ATTN_GROUPS = ((128, 1), (512, 4), (2048, 16))
N_ATTN_HEADS = HEADS_PER_GROUP * len(ATTN_GROUPS)
ATTN_OUT_W = HEADS_PER_GROUP * HEAD_DIM
ROPE_DIM = HEAD_DIM // 4
ROPE_THETA = 500000.0
BLOCK = 128
SSM_CH = 16
SSM_GROUPS = 32
SSM_W = SSM_CH * SSM_GROUPS
SSM_STATE = 64
SSM_LANES = SSM_GROUPS * SSM_STATE
D_FF = 2816
QK_W = 2 * N_ATTN_HEADS * HEAD_DIM
QKV_W = 3 * N_ATTN_HEADS * HEAD_DIM
QKVU_W = QKV_W + SSM_W
RMS_EPS = 1e-6
NEG_INF = -1e30

LANE = 128
PERM_ROWS = 256
VMEM_LIMIT = 56 * 1024 * 1024

TS_INPROJ = 1024
TM_MIX = 1024
TM_FFN = 1024
FFN_SPLIT = 4
MIX_SPLIT = 4
INPROJ_SPLIT = 4
SSM_TC = 32
ATTN_UNROLL = 16
SSM_OUT_TILE = 2 * LANE
SSM_PAIR = 2
SUBLANE = 8


def _rmsnorm(x, g):
    return x * lax.rsqrt(jnp.mean(x * x, axis=-1, keepdims=True) + RMS_EPS) * g


def _const_spec(shape):
    return pl.BlockSpec(shape, lambda *_: (0,) * len(shape), pipeline_mode=pl.Buffered(1))


def _inproj_kernel(x_ref, g_ref, w_ref, rc_ref, ra_ref, rb_ref, qkv_ref, u_ref):
    ts = x_ref.shape[1]
    sub = ts // INPROJ_SPLIT
    half = ROPE_DIM // 2
    for r0 in range(0, ts, sub):
        rs = slice(r0, r0 + sub)
        h = _rmsnorm(x_ref[0, rs, :], g_ref[...]).astype(BF16)
        p = jnp.dot(h, w_ref[...], preferred_element_type=F32)
        rc, ra, rb = rc_ref[rs, :], ra_ref[rs, :], rb_ref[rs, :]
        for c in range(QK_W // LANE):
            t = p[:, c * LANE:(c + 1) * LANE]
            t = t * rc + pltpu.roll(t, LANE - half, 1) * ra + pltpu.roll(t, half, 1) * rb
            qkv_ref[0, rs, c * LANE:(c + 1) * LANE] = t.astype(BF16)
        qkv_ref[0, rs, QK_W:QKV_W] = p[:, QK_W:QKV_W].astype(BF16)
        u_ref[0, rs, :] = p[:, QKV_W:QKVU_W]


def _rope_tables(S):
    half = ROPE_DIM // 2
    pos = np.arange(S, dtype=np.float64)
    inv = np.power(np.float64(ROPE_THETA), -np.arange(half, dtype=np.float64) * 2.0 / ROPE_DIM)
    ang = pos[:, None] * inv[None, :]
    cos, sin = np.cos(ang), np.sin(ang)
    ones = np.ones((S, HEAD_DIM - ROPE_DIM))
    zeros_h = np.zeros((S, half))
    zeros_r = np.zeros((S, HEAD_DIM - ROPE_DIM))
    per_head_c = np.concatenate([cos, cos, ones], axis=1)
    per_head_a = np.concatenate([-sin, zeros_h, zeros_r], axis=1)
    per_head_b = np.concatenate([zeros_h, sin, zeros_r], axis=1)
    rep = LANE // HEAD_DIM
    return tuple(jnp.asarray(np.tile(t, (1, rep)), dtype=F32)
                 for t in (per_head_c, per_head_a, per_head_b))


def _inproj(x, g, w, rope):
    B, S, D = x.shape
    ts = TS_INPROJ
    tab_spec = pl.BlockSpec((ts, LANE), lambda b, i: (i, 0))
    return pl.pallas_call(
        _inproj_kernel,
        grid=(B, S // ts),
        in_specs=[
            pl.BlockSpec((1, ts, D), lambda b, i: (b, i, 0)),
            _const_spec((1, D)),
            _const_spec((D, QKVU_W)),
            tab_spec, tab_spec, tab_spec,
        ],
        out_specs=[
            pl.BlockSpec((1, ts, QKV_W), lambda b, i: (b, i, 0)),
            pl.BlockSpec((1, ts, SSM_W), lambda b, i: (b, i, 0)),
        ],
        out_shape=[
            jax.ShapeDtypeStruct((B, S, QKV_W), BF16),
            jax.ShapeDtypeStruct((B, S, SSM_W), F32),
        ],
        compiler_params=pltpu.CompilerParams(
            dimension_semantics=("arbitrary", "arbitrary"), vmem_limit_bytes=VMEM_LIMIT),
        name="inproj",
    )(x, g, w, *rope)


def _split3(x):
    hi = x.astype(BF16)
    r1 = x - hi.astype(F32)
    mid = r1.astype(BF16)
    lo = (r1 - mid.astype(F32)).astype(BF16)
    return hi, mid, lo


def _head_combine(per_head, lane_lo):
    return jnp.concatenate(
        [jnp.where(lane_lo, per_head[0], per_head[1]), jnp.where(lane_lo, per_head[2], per_head[3])],
        axis=1)


def _attn_block(q, k, v, bias, hm, lane_lo):
    nh = HEADS_PER_GROUP
    qm = jnp.concatenate([q * hm[h:h + 1, :] for h in range(nh)], axis=0)
    s = lax.dot_general(qm, k, (((1,), (1,)), ((), ())), preferred_element_type=F32)
    pv_h, m_h, l_h = [], [], []
    for h in range(nh):
        sh = s[h * BLOCK:(h + 1) * BLOCK] + bias
        m = jnp.max(sh, axis=-1, keepdims=True)
        p = jnp.exp(sh - m)
        l = jnp.sum(p, axis=-1, keepdims=True)
        pv = jnp.dot(p.astype(BF16), v, preferred_element_type=F32)
        pv_h.append(pv[:, (h // 2) * LANE:(h // 2 + 1) * LANE])
        m_h.append(jnp.broadcast_to(m, (BLOCK, LANE)))
        l_h.append(jnp.broadcast_to(l, (BLOCK, LANE)))
    l_b = _head_combine(l_h, lane_lo)
    o = _head_combine(pv_h, lane_lo) * (1.0 / l_b)
    return o, _head_combine(m_h, lane_lo) + jnp.log(l_b)


def _attn_kernel(qkv_ref, hm_ref, pf4_ref, pf16_ref, o_ref, qs, ks, vs, o_scr, l_scr, bias_scr):
    S = qkv_ref.shape[1]
    n_chunks = ATTN_OUT_W // LANE
    perms = {4: pf4_ref, 16: pf16_ref}
    lane_lo = lax.broadcasted_iota(jnp.int32, (BLOCK, LANE), 1) < HEAD_DIM
    qi = lax.broadcasted_iota(jnp.int32, (BLOCK, 2 * BLOCK), 0)
    ki = lax.broadcasted_iota(jnp.int32, (BLOCK, 2 * BLOCK), 1)
    dist = qi + BLOCK - ki
    band = (dist >= 0) & (dist <= BLOCK)
    bias_scr[1] = jnp.where(band, 0.0, NEG_INF)
    bias_scr[0] = jnp.where(band & (ki >= BLOCK), 0.0, NEG_INF)
    hm = hm_ref[...]

    for gi, (window, d) in enumerate(ATTN_GROUPS):
        L = S // d
        nb = L // BLOCK
        cols = [part * N_ATTN_HEADS * HEAD_DIM + gi * ATTN_OUT_W for part in range(3)]
        plen = PERM_ROWS // d
        ppb = BLOCK // plen
        if d > 1:
            pf = perms[d][...]
            for scr, col in zip((qs, ks, vs), cols):
                for mblk in range(S // PERM_ROWS):
                    rs = slice(mblk * PERM_ROWS, (mblk + 1) * PERM_ROWS)
                    blk = qkv_ref[0, rs, col:col + ATTN_OUT_W]
                    scr[rs, :] = jnp.dot(pf, blk, preferred_element_type=F32).astype(BF16)

        def piece_rows(r, n, jj, d=d, plen=plen, ppb=ppb):
            if d == 1:
                return pl.ds(pl.multiple_of(n * BLOCK, BLOCK), BLOCK)
            return pl.ds(pl.multiple_of((n * ppb + jj) * PERM_ROWS + r * plen, plen), plen)

        def load(scr, col, r, n, d=d, ppb=ppb, piece_rows=piece_rows):
            if d == 1:
                return qkv_ref[0, piece_rows(r, n, 0), col:col + ATTN_OUT_W]
            return jnp.concatenate([scr[piece_rows(r, n, jj), :] for jj in range(ppb)], axis=0)

        def body(idx, carry, gi=gi, d=d, nb=nb, cols=cols, load=load, plen=plen, ppb=ppb,
                 piece_rows=piece_rows):
            r = idx // nb
            n = idx % nb
            q = load(qs, cols[0], r, n)
            k = load(ks, cols[1], r, n)
            v = load(vs, cols[2], r, n)
            if nb > 1:
                prev = jnp.maximum(n - 1, 0)
                k = jnp.concatenate([load(ks, cols[1], r, prev), k], axis=0)
                v = jnp.concatenate([load(vs, cols[2], r, prev), v], axis=0)
                bias = bias_scr[jnp.minimum(n, 1)]
            else:
                bias = bias_scr[1, :, BLOCK:]
            o, lse_b = _attn_block(q, k, v, bias, hm, lane_lo)
            start = r + d * BLOCK * n
            rows = pl.ds(pl.multiple_of(start, BLOCK), BLOCK) if d == 1 else pl.ds(start, BLOCK, stride=d)
            for c in range(n_chunks):
                o_scr[gi, c, rows, :] = o[:, c * LANE:(c + 1) * LANE]
                l_scr[gi, c, rows, :] = lse_b[:, c * LANE:(c + 1) * LANE]
            return carry

        lax.fori_loop(0, d * nb, body, 0, unroll=ATTN_UNROLL)

    n_groups = len(ATTN_GROUPS)

    def merge(i, carry):
        rs = pl.ds(pl.multiple_of(i * PERM_ROWS, PERM_ROWS), PERM_ROWS)
        for c in range(n_chunks):
            ls = [l_scr[g, c, rs, :] for g in range(n_groups)]
            m = functools.reduce(jnp.maximum, ls)
            es = [jnp.exp(l - m) for l in ls]
            inv_den = 1.0 / functools.reduce(lambda a, b: a + b, es)
            acc = None
            for g in range(n_groups):
                term = (es[g] * inv_den) * o_scr[g, c, rs, :]
                acc = term if acc is None else acc + term
            o_ref[0, rs, c * LANE:(c + 1) * LANE] = acc.astype(o_ref.dtype)
        return carry

    lax.fori_loop(0, S // PERM_ROWS, merge, 0)


def _residue_permutation(d):
    dst = np.arange(PERM_ROWS)
    plen = PERM_ROWS // d
    src = (dst % plen) * d + dst // plen
    return jnp.asarray(src[:, None] == np.arange(PERM_ROWS)[None, :], dtype=BF16)


def _attn(qkv):
    B, S, _ = qkv.shape
    n_groups = len(ATTN_GROUPS)
    for window, d in ATTN_GROUPS:
        assert window // d == BLOCK and S % (d * BLOCK) == 0
        assert d == 1 or (PERM_ROWS % d == 0 and BLOCK % (PERM_ROWS // d) == 0 and S % PERM_ROWS == 0)
    lane_head = np.arange(ATTN_OUT_W) // HEAD_DIM
    head = np.arange(HEADS_PER_GROUP)
    hm = jnp.asarray(np.where(head[:, None] == lane_head[None, :], HEAD_DIM ** -0.5, 0.0), dtype=BF16)
    n_chunks = ATTN_OUT_W // LANE
    perm_mats = [_residue_permutation(d) for _, d in ATTN_GROUPS if d > 1]
    return pl.pallas_call(
        _attn_kernel,
        grid=(B,),
        in_specs=[pl.BlockSpec((1, S, QKV_W), lambda b: (b, 0, 0)), _const_spec(hm.shape)]
        + [_const_spec(m.shape) for m in perm_mats],
        out_specs=pl.BlockSpec((1, S, ATTN_OUT_W), lambda b: (b, 0, 0)),
        out_shape=jax.ShapeDtypeStruct((B, S, ATTN_OUT_W), BF16),
        scratch_shapes=[
            pltpu.VMEM((S, ATTN_OUT_W), BF16),
            pltpu.VMEM((S, ATTN_OUT_W), BF16),
            pltpu.VMEM((S, ATTN_OUT_W), BF16),
            pltpu.VMEM((n_groups, n_chunks, S, LANE), F32),
            pltpu.VMEM((n_groups, n_chunks, S, LANE), F32),
            pltpu.VMEM((2, BLOCK, 2 * BLOCK), F32),
        ],
        compiler_params=pltpu.CompilerParams(
            dimension_semantics=("arbitrary",), vmem_limit_bytes=VMEM_LIMIT),
        name="attn",
    )(qkv, hm, *perm_mats)


def _ssm_kernel(u_ref, p_ref, pt_ref, wbu_ref, wc_ref, a_ref, d_ref, y_ref,
                bu_scr, x_scr, st_scr, us_scr):
    nb, tc, _ = u_ref.shape
    rows = nb * tc
    n_half = nb // SUBLANE
    hrows = SUBLANE * tc
    n_chunks = SSM_LANES // LANE
    tile_n = 2 * LANE
    assert SSM_PAIR * LANE == tile_n, "one scan pass covers the state lanes of one matmul tile"

    @pl.when(pl.program_id(0) == 0)
    def _():
        st_scr[...] = jnp.zeros_like(st_scr)
        bu_scr[...] = jnp.zeros_like(bu_scr)
        x_scr[...] = jnp.zeros_like(x_scr)
        us_scr[1] = jnp.zeros(us_scr.shape[1:], F32)
        us_scr[2] = jnp.zeros(us_scr.shape[1:], F32)

    us_scr[0] = us_scr[1]
    us_scr[1] = us_scr[2]

    def permute(pm, x):
        return jnp.concatenate(
            [jnp.dot(pm, x[h * hrows:(h + 1) * hrows], preferred_element_type=F32)
             for h in range(n_half)], axis=0)

    u_hi, u_mid, u_lo = _split3(u_ref[...].reshape(rows, SSM_W))
    p_fwd = p_ref[...]
    ut_hi = permute(p_fwd, u_hi)
    us_scr[2] = ut_hi + permute(p_fwd, u_mid) + permute(p_fwd, u_lo)
    ub = ut_hi.astype(BF16)

    out_tile = SSM_OUT_TILE
    k_chunks = (out_tile // SSM_CH * SSM_STATE) // LANE
    cp_per_out = k_chunks // SSM_PAIR
    ys = []

    for cp in range(n_chunks // SSM_PAIR):
        if cp % cp_per_out == 0:
            j = cp // cp_per_out
            acc = None
            for ri in range(2):
                xs = jnp.concatenate(
                    [x_scr[ri, j * k_chunks + c] for c in range(k_chunks)], axis=1)
                part = jnp.dot(xs, wc_ref[ri, j], preferred_element_type=F32)
                acc = part if acc is None else acc + part
            ys.append(acc)
        chains = [(cp * SSM_PAIR + j, h) for j in range(SSM_PAIR) for h in range(n_half)]
        a_re = {c: a_ref[0, c] for c, _ in chains}
        a_im = {c: a_ref[1, c] for c, _ in chains}
        carry = [st_scr[ri, c, h] for c, h in chains for ri in range(2)]
        pack = 4 // x_scr.dtype.itemsize
        for t0 in range(0, tc, pack):
            for j, (c, h) in enumerate(chains):
                xr, xi = carry[2 * j], carry[2 * j + 1]
                outs_r, outs_i = [], []
                for t in range(t0, t0 + pack):
                    row = pl.ds(h * hrows + SUBLANE * t, SUBLANE)
                    xr, xi = (a_re[c] * xr - a_im[c] * xi + bu_scr[0, c, row, :],
                              a_re[c] * xi + a_im[c] * xr + bu_scr[1, c, row, :])
                    outs_r.append(xr)
                    outs_i.append(xi)
                rows_p = pl.ds(h * hrows + SUBLANE * t0, SUBLANE * pack)
                x_scr[0, c, rows_p, :] = jnp.concatenate(outs_r, axis=0).astype(x_scr.dtype)
                x_scr[1, c, rows_p, :] = jnp.concatenate(outs_i, axis=0).astype(x_scr.dtype)
                carry[2 * j], carry[2 * j + 1] = xr, xi
        for j, (c, h) in enumerate(chains):
            st_scr[0, c, h] = carry[2 * j]
            st_scr[1, c, h] = carry[2 * j + 1]
        k0 = (cp * tile_n // SSM_STATE * SSM_CH) // LANE * LANE
        for ri in range(2):
            bu = jnp.dot(ub[:, k0:k0 + LANE], wbu_ref[ri, cp], preferred_element_type=F32)
            for c in range(SSM_PAIR):
                bu_scr[ri, cp * SSM_PAIR + c] = bu[:, c * LANE:(c + 1) * LANE]

    y = jnp.concatenate(ys, axis=1) + d_ref[...] * us_scr[0]
    y = jax.nn.gelu(y).astype(BF16)
    y = permute(pt_ref[...], y).astype(y_ref.dtype)
    y_ref[...] = y.reshape(nb, tc, SSM_W)


def _ssm_params(a_re, a_im, log_dt, b_re, b_im, c_re, c_im, d_skip):
    lr, li = a_re.astype(F32), a_im.astype(F32)
    dt = jnp.exp(log_dt.astype(F32))[:, None]
    mag = jnp.exp(lr * dt)
    ab_re, ab_im = mag * jnp.cos(li * dt), mag * jnp.sin(li * dt)
    den = lr * lr + li * li
    nr, ni = ab_re - 1.0, ab_im
    f_re = (nr * lr + ni * li) / den
    f_im = (ni * lr - nr * li) / den
    br, bi = b_re.astype(F32), b_im.astype(F32)
    bb_re = f_re[..., None] * br - f_im[..., None] * bi
    bb_im = f_re[..., None] * bi + f_im[..., None] * br
    tile_n = 2 * LANE
    n_bu = SSM_LANES // tile_n
    n_c = SSM_W // SSM_OUT_TILE
    k_rows = SSM_OUT_TILE // SSM_CH * SSM_STATE

    def bu_tiles(bb):
        rows = jnp.transpose(bb, (0, 2, 1)).reshape(SSM_W // LANE, LANE, SSM_STATE)
        k_slice = (np.arange(n_bu) * tile_n // SSM_STATE * SSM_CH) // LANE
        tiled = jnp.tile(rows[k_slice], (1, 1, tile_n // SSM_STATE))
        g_row = k_slice[:, None, None] * (LANE // SSM_CH) + np.arange(LANE)[None, :, None] // SSM_CH
        g_col = (np.arange(n_bu)[:, None, None] * tile_n + np.arange(tile_n)[None, None, :]) // SSM_STATE
        return jnp.where(g_row == g_col, tiled, 0.0)

    def c_tiles(c):
        rows = jnp.transpose(c, (0, 2, 1)).reshape(n_c, k_rows, SSM_CH)
        tiled = jnp.tile(rows, (1, 1, SSM_OUT_TILE // SSM_CH))
        same_group = ((np.arange(k_rows)[:, None] // SSM_STATE)
                      == (np.arange(SSM_OUT_TILE)[None, :] // SSM_CH))
        return jnp.where(same_group[None], tiled, 0.0)

    wbu = jnp.stack([bu_tiles(bb_re), bu_tiles(bb_im)]).astype(BF16)
    wc = jnp.stack([c_tiles(c_re.astype(F32)), c_tiles(-c_im.astype(F32))]).astype(BF16)
    n_chunks = SSM_LANES // LANE
    a = jnp.stack([ab_re.reshape(n_chunks, 1, LANE), ab_im.reshape(n_chunks, 1, LANE)])
    a = jnp.broadcast_to(a, (2, n_chunks, SUBLANE, LANE))
    d = d_skip.astype(F32).reshape(1, SSM_W)
    return wbu, wc, a, d


def _ssm_row_permutation(tc):
    dst = np.arange(SUBLANE * tc)
    src = (dst % SUBLANE) * tc + dst // SUBLANE
    p = src[:, None] == np.arange(SUBLANE * tc)[None, :]
    return jnp.asarray(p, dtype=BF16), jnp.asarray(p.T, dtype=BF16)


def _ssm(u, wbu, wc, a, d):
    B, S, _ = u.shape
    tc = SSM_TC
    n_chunks = SSM_LANES // LANE
    assert B % SUBLANE == 0 and S % tc == 0 and tc % 2 == 0
    p_fwd, p_bwd = _ssm_row_permutation(tc)
    n_steps = S // tc
    return pl.pallas_call(
        _ssm_kernel,
        grid=(n_steps + 2,),
        in_specs=[
            pl.BlockSpec((B, tc, SSM_W), lambda i: (0, jnp.minimum(i, n_steps - 1), 0)),
            _const_spec(p_fwd.shape),
            _const_spec(p_bwd.shape),
            _const_spec(wbu.shape),
            _const_spec(wc.shape),
            _const_spec(a.shape),
            _const_spec(d.shape),
        ],
        out_specs=pl.BlockSpec((B, tc, SSM_W), lambda i: (0, jnp.maximum(i - 2, 0), 0)),
        out_shape=jax.ShapeDtypeStruct((B, S, SSM_W), BF16),
        scratch_shapes=[
            pltpu.VMEM((2, n_chunks, B * tc, LANE), F32),
            pltpu.VMEM((2, n_chunks, B * tc, LANE), BF16),
            pltpu.VMEM((2, n_chunks, B // SUBLANE, SUBLANE, LANE), F32),
            pltpu.VMEM((3, B * tc, SSM_W), F32),
        ],
        compiler_params=pltpu.CompilerParams(
            dimension_semantics=("arbitrary",), vmem_limit_bytes=VMEM_LIMIT),
        name="ssm",
    )(u, p_fwd, p_bwd, wbu, wc, a, d)


def _mix_kernel(x_ref, attn_ref, y_ref, g_ref, wg_ref, wao_ref, wglu_ref, wout_ref, o_ref):
    tm = x_ref.shape[1]
    sub = tm // MIX_SPLIT
    for r0 in range(0, tm, sub):
        rs = slice(r0, r0 + sub)
        x = x_ref[0, rs, :]
        h = _rmsnorm(x, g_ref[...]).astype(BF16)
        gate = jax.nn.sigmoid(jnp.dot(h, wg_ref[...], preferred_element_type=F32))
        attn_d = jnp.dot(attn_ref[0, rs, :], wao_ref[...], preferred_element_type=F32)
        z = jnp.dot(y_ref[0, rs, :], wglu_ref[...], preferred_element_type=F32)
        ssm_out = z[:, :D_MODEL] * jax.nn.sigmoid(z[:, D_MODEL:])
        merged = gate[:, :D_MODEL] * attn_d + gate[:, D_MODEL:] * ssm_out
        o_ref[0, rs, :] = x + jnp.dot(merged.astype(BF16), wout_ref[...], preferred_element_type=F32)


def _mix(x, attn, y, g, wg, wao, wglu, wout):
    B, S, D = x.shape
    tm = TM_MIX
    tok = lambda w: pl.BlockSpec((1, tm, w), lambda b, i: (b, i, 0))
    return pl.pallas_call(
        _mix_kernel,
        grid=(B, S // tm),
        in_specs=[tok(D), tok(ATTN_OUT_W), tok(SSM_W), _const_spec(g.shape), _const_spec(wg.shape),
                  _const_spec(wao.shape), _const_spec(wglu.shape), _const_spec(wout.shape)],
        out_specs=tok(D),
        out_shape=jax.ShapeDtypeStruct((B, S, D), F32),
        compiler_params=pltpu.CompilerParams(
            dimension_semantics=("arbitrary", "arbitrary"), vmem_limit_bytes=VMEM_LIMIT),
        name="mix",
    )(x, attn, y, g, wg, wao, wglu, wout)


def _ffn_kernel(x_ref, g2_ref, gf_ref, wgate_ref, wup_ref, wdown_ref, o_ref):
    tm = x_ref.shape[1]
    sub = tm // FFN_SPLIT
    for r0 in range(0, tm, sub):
        rs = slice(r0, r0 + sub)
        x = x_ref[0, rs, :]
        h = _rmsnorm(x, g2_ref[...]).astype(BF16)
        gate = jnp.dot(h, wgate_ref[...], preferred_element_type=F32)
        up = jnp.dot(h, wup_ref[...], preferred_element_type=F32)
        act = (jax.nn.silu(gate) * up).astype(BF16)
        ff = jnp.dot(act, wdown_ref[...], preferred_element_type=F32)
        o_ref[0, rs, :] = _rmsnorm(x + ff, gf_ref[...])


def _ffn(x, g2, gf, wgate, wup, wdown):
    B, S, D = x.shape
    tm = TM_FFN
    tok = pl.BlockSpec((1, tm, D), lambda b, i: (b, i, 0))
    return pl.pallas_call(
        _ffn_kernel,
        grid=(B, S // tm),
        in_specs=[tok, _const_spec(g2.shape), _const_spec(gf.shape), _const_spec(wgate.shape),
                  _const_spec(wup.shape), _const_spec(wdown.shape)],
        out_specs=tok,
        out_shape=jax.ShapeDtypeStruct((B, S, D), F32),
        compiler_params=pltpu.CompilerParams(
            dimension_semantics=("arbitrary", "arbitrary"), vmem_limit_bytes=VMEM_LIMIT),
        name="ffn",
    )(x, g2, gf, wgate, wup, wdown)


def kernel(x, norm_mix_g, w_in, ssm_a_re, ssm_a_im, ssm_log_dt, ssm_b_re, ssm_b_im, ssm_c_re, ssm_c_im, ssm_d, w_glu, w_attn_out, w_out, norm_ffn_g, w_ffn_gate, w_ffn_up, w_ffn_down, norm_final_g):
    B, S, D = x.shape
    assert D == D_MODEL and norm_mix_g.shape[0] == 1, "single-layer block expected"
    layer = 0
    g_mix = norm_mix_g[layer].reshape(1, D).astype(F32)
    w_in_l = w_in[layer]
    w_qkvu = w_in_l[:, :QKVU_W].astype(BF16)
    w_gate = w_in_l[:, QKVU_W:].astype(BF16)

    qkv, u = _inproj(x, g_mix, w_qkvu, _rope_tables(S))
    attn = _attn(qkv)
    wbu, wc, a, d = _ssm_params(
        ssm_a_re[layer], ssm_a_im[layer], ssm_log_dt[layer], ssm_b_re[layer], ssm_b_im[layer],
        ssm_c_re[layer], ssm_c_im[layer], ssm_d[layer])
    y = _ssm(u, wbu, wc, a, d)
    x1 = _mix(x, attn, y, g_mix, w_gate, w_attn_out[layer].astype(BF16),
              w_glu[layer].astype(BF16), w_out[layer].astype(BF16))
    return _ffn(x1, norm_ffn_g[layer].reshape(1, D).astype(F32),
                norm_final_g.reshape(1, D).astype(F32), w_ffn_gate[layer].astype(BF16),
                w_ffn_up[layer].astype(BF16), w_ffn_down[layer].astype(BF16))
```

```python
import functools

import jax
import jax.numpy as jnp
import numpy as np
from jax import lax
from jax.experimental import pallas as pl
from jax.experimental.pallas import tpu as pltpu

F32 = jnp.float32
BF16 = jnp.bfloat16

D_MODEL = 1024
HEAD_DIM = 64
HEADS_PER_GROUP = 4
ATTN_GROUPS = ((128, 1), (512, 4), (2048, 16))
N_ATTN_HEADS = HEADS_PER_GROUP * len(ATTN_GROUPS)
ATTN_OUT_W = HEADS_PER_GROUP * HEAD_DIM
ROPE_DIM = HEAD_DIM // 4
ROPE_THETA = 500000.0
BLOCK = 128
SSM_CH = 16
SSM_GROUPS = 32
SSM_W = SSM_CH * SSM_GROUPS
SSM_STATE = 64
SSM_LANES = SSM_GROUPS * SSM_STATE
D_FF = 2816
QK_W = 2 * N_ATTN_HEADS * HEAD_DIM
QKV_W = 3 * N_ATTN_HEADS * HEAD_DIM
QKVU_W = QKV_W + SSM_W
RMS_EPS = 1e-6
NEG_INF = -1e30

LANE = 128
PERM_ROWS = 256
VMEM_LIMIT = 56 * 1024 * 1024

TS_INPROJ = 2048
TM_MIX = 2048
TM_FFN = 2048
FFN_SPLIT = 8
MIX_SPLIT = 8
INPROJ_SPLIT = 8
SSM_TC = 32
ATTN_UNROLL = 16
SSM_OUT_TILE = 2 * LANE
SSM_PAIR = 2
SUBLANE = 8


def _rmsnorm(x, g):
    return x * lax.rsqrt(jnp.mean(x * x, axis=-1, keepdims=True) + RMS_EPS) * g


def _const_spec(shape):
    return pl.BlockSpec(shape, lambda *_: (0,) * len(shape), pipeline_mode=pl.Buffered(1))


def _inproj_kernel(x_ref, g_ref, w_ref, rc_ref, ra_ref, rb_ref, qkv_ref, u_ref):
    ts = x_ref.shape[1]
    sub = ts // INPROJ_SPLIT
    half = ROPE_DIM // 2
    for r0 in range(0, ts, sub):
        rs = slice(r0, r0 + sub)
        h = _rmsnorm(x_ref[0, rs, :], g_ref[...]).astype(BF16)
        p = jnp.dot(h, w_ref[...], preferred_element_type=F32)
        rc, ra, rb = rc_ref[rs, :], ra_ref[rs, :], rb_ref[rs, :]
        for c in range(QK_W // LANE):
            t = p[:, c * LANE:(c + 1) * LANE]
            t = t * rc + pltpu.roll(t, LANE - half, 1) * ra + pltpu.roll(t, half, 1) * rb
            qkv_ref[0, rs, c * LANE:(c + 1) * LANE] = t.astype(BF16)
        qkv_ref[0, rs, QK_W:QKV_W] = p[:, QK_W:QKV_W].astype(BF16)
        u_ref[0, rs, :] = p[:, QKV_W:QKVU_W]


def _rope_tables(S):
    half = ROPE_DIM // 2
    pos = np.arange(S, dtype=np.float64)
    inv = np.power(np.float64(ROPE_THETA), -np.arange(half, dtype=np.float64) * 2.0 / ROPE_DIM)
    ang = pos[:, None] * inv[None, :]
    cos, sin = np.cos(ang), np.sin(ang)
    ones = np.ones((S, HEAD_DIM - ROPE_DIM))
    zeros_h = np.zeros((S, half))
    zeros_r = np.zeros((S, HEAD_DIM - ROPE_DIM))
    per_head_c = np.concatenate([cos, cos, ones], axis=1)
    per_head_a = np.concatenate([-sin, zeros_h, zeros_r], axis=1)
    per_head_b = np.concatenate([zeros_h, sin, zeros_r], axis=1)
    rep = LANE // HEAD_DIM
    return tuple(jnp.asarray(np.tile(t, (1, rep)), dtype=F32)
                 for t in (per_head_c, per_head_a, per_head_b))


def _inproj(x, g, w, rope):
    B, S, D = x.shape
    ts = TS_INPROJ
    tab_spec = pl.BlockSpec((ts, LANE), lambda b, i: (i, 0))
    return pl.pallas_call(
        _inproj_kernel,
        grid=(B, S // ts),
        in_specs=[
            pl.BlockSpec((1, ts, D), lambda b, i: (b, i, 0)),
            _const_spec((1, D)),
            pl.BlockSpec((D, QKVU_W), lambda b, i: (0, 0), pipeline_mode=pl.Buffered(1)),
            tab_spec, tab_spec, tab_spec,
        ],
        out_specs=[
            pl.BlockSpec((1, ts, QKV_W), lambda b, i: (b, i, 0)),
            pl.BlockSpec((1, ts, SSM_W), lambda b, i: (b, i, 0)),
        ],
        out_shape=[
            jax.ShapeDtypeStruct((B, S, QKV_W), BF16),
            jax.ShapeDtypeStruct((B, S, SSM_W), F32),
        ],
        compiler_params=pltpu.CompilerParams(
            dimension_semantics=("arbitrary", "arbitrary"), vmem_limit_bytes=VMEM_LIMIT),
        name="inproj",
    )(x, g, w, *rope)


def _split3(x):
    hi = x.astype(BF16)
    r1 = x - hi.astype(F32)
    mid = r1.astype(BF16)
    lo = (r1 - mid.astype(F32)).astype(BF16)
    return hi, mid, lo


def _head_combine(per_head, lane_lo):
    return jnp.concatenate(
        [jnp.where(lane_lo, per_head[0], per_head[1]), jnp.where(lane_lo, per_head[2], per_head[3])],
        axis=1)


def _attn_block(q, k, v, bias, hm, lane_lo):
    nh = HEADS_PER_GROUP
    qm = jnp.concatenate([q * hm[h:h + 1, :] for h in range(nh)], axis=0)
    s = lax.dot_general(qm, k, (((1,), (1,)), ((), ())), preferred_element_type=F32)
    pv_h, m_h, l_h = [], [], []
    for h in range(nh):
        sh = s[h * BLOCK:(h + 1) * BLOCK] + bias
        m = jnp.max(sh, axis=-1, keepdims=True)
        p = jnp.exp(sh - m)
        l = jnp.sum(p, axis=-1, keepdims=True)
        pv = jnp.dot(p.astype(BF16), v, preferred_element_type=F32)
        pv_h.append(pv[:, (h // 2) * LANE:(h // 2 + 1) * LANE])
        m_h.append(jnp.broadcast_to(m, (BLOCK, LANE)))
        l_h.append(jnp.broadcast_to(l, (BLOCK, LANE)))
    l_b = _head_combine(l_h, lane_lo)
    o = _head_combine(pv_h, lane_lo) * (1.0 / l_b)
    return o, _head_combine(m_h, lane_lo) + jnp.log(l_b)


def _attn_kernel(qkv_ref, hm_ref, pf4_ref, pf16_ref, o_ref, qs, ks, vs, o_scr, l_scr, bias_scr):
    S = qkv_ref.shape[1]
    n_chunks = ATTN_OUT_W // LANE
    perms = {4: pf4_ref, 16: pf16_ref}
    lane_lo = lax.broadcasted_iota(jnp.int32, (BLOCK, LANE), 1) < HEAD_DIM
    qi = lax.broadcasted_iota(jnp.int32, (BLOCK, 2 * BLOCK), 0)
    ki = lax.broadcasted_iota(jnp.int32, (BLOCK, 2 * BLOCK), 1)
    dist = qi + BLOCK - ki
    band = (dist >= 0) & (dist <= BLOCK)
    bias_scr[1] = jnp.where(band, 0.0, NEG_INF)
    bias_scr[0] = jnp.where(band & (ki >= BLOCK), 0.0, NEG_INF)
    hm = hm_ref[...]

    for gi, (window, d) in enumerate(ATTN_GROUPS):
        L = S // d
        nb = L // BLOCK
        cols = [part * N_ATTN_HEADS * HEAD_DIM + gi * ATTN_OUT_W for part in range(3)]
        plen = PERM_ROWS // d
        ppb = BLOCK // plen
        if d > 1:
            pf = perms[d][...]
            for scr, col in zip((qs, ks, vs), cols):
                for mblk in range(S // PERM_ROWS):
                    rs = slice(mblk * PERM_ROWS, (mblk + 1) * PERM_ROWS)
                    blk = qkv_ref[0, rs, col:col + ATTN_OUT_W]
                    scr[rs, :] = jnp.dot(pf, blk, preferred_element_type=F32).astype(BF16)

        def piece_rows(r, n, jj, d=d, plen=plen, ppb=ppb):
            if d == 1:
                return pl.ds(pl.multiple_of(n * BLOCK, BLOCK), BLOCK)
            return pl.ds(pl.multiple_of((n * ppb + jj) * PERM_ROWS + r * plen, plen), plen)

        def load(scr, col, r, n, d=d, ppb=ppb, piece_rows=piece_rows):
            if d == 1:
                return qkv_ref[0, piece_rows(r, n, 0), col:col + ATTN_OUT_W]
            return jnp.concatenate([scr[piece_rows(r, n, jj), :] for jj in range(ppb)], axis=0)

        def body(idx, carry, gi=gi, d=d, nb=nb, cols=cols, load=load, plen=plen, ppb=ppb,
                 piece_rows=piece_rows):
            r = idx // nb
            n = idx % nb
            q = load(qs, cols[0], r, n)
            k = load(ks, cols[1], r, n)
            v = load(vs, cols[2], r, n)
            if nb > 1:
                prev = jnp.maximum(n - 1, 0)
                k = jnp.concatenate([load(ks, cols[1], r, prev), k], axis=0)
                v = jnp.concatenate([load(vs, cols[2], r, prev), v], axis=0)
                bias = bias_scr[jnp.minimum(n, 1)]
            else:
                bias = bias_scr[1, :, BLOCK:]
            o, lse_b = _attn_block(q, k, v, bias, hm, lane_lo)
            start = r + d * BLOCK * n
            rows = pl.ds(pl.multiple_of(start, BLOCK), BLOCK) if d == 1 else pl.ds(start, BLOCK, stride=d)
            for c in range(n_chunks):
                o_scr[gi, c, rows, :] = o[:, c * LANE:(c + 1) * LANE]
                l_scr[gi, c, rows, :] = lse_b[:, c * LANE:(c + 1) * LANE]
            return carry

        lax.fori_loop(0, d * nb, body, 0, unroll=ATTN_UNROLL)

    n_groups = len(ATTN_GROUPS)

    def merge(i, carry):
        rs = pl.ds(pl.multiple_of(i * PERM_ROWS, PERM_ROWS), PERM_ROWS)
        for c in range(n_chunks):
            ls = [l_scr[g, c, rs, :] for g in range(n_groups)]
            m = functools.reduce(jnp.maximum, ls)
            es = [jnp.exp(l - m) for l in ls]
            inv_den = 1.0 / functools.reduce(lambda a, b: a + b, es)
            acc = None
            for g in range(n_groups):
                term = (es[g] * inv_den) * o_scr[g, c, rs, :]
                acc = term if acc is None else acc + term
            o_ref[0, rs, c * LANE:(c + 1) * LANE] = acc.astype(o_ref.dtype)
        return carry

    lax.fori_loop(0, S // PERM_ROWS, merge, 0)


def _residue_permutation(d):
    dst = np.arange(PERM_ROWS)
    plen = PERM_ROWS // d
    src = (dst % plen) * d + dst // plen
    return jnp.asarray(src[:, None] == np.arange(PERM_ROWS)[None, :], dtype=BF16)


def _attn(qkv):
    B, S, _ = qkv.shape
    n_groups = len(ATTN_GROUPS)
    for window, d in ATTN_GROUPS:
        assert window // d == BLOCK and S % (d * BLOCK) == 0
        assert d == 1 or (PERM_ROWS % d == 0 and BLOCK % (PERM_ROWS // d) == 0 and S % PERM_ROWS == 0)
    lane_head = np.arange(ATTN_OUT_W) // HEAD_DIM
    head = np.arange(HEADS_PER_GROUP)
    hm = jnp.asarray(np.where(head[:, None] == lane_head[None, :], HEAD_DIM ** -0.5, 0.0), dtype=BF16)
    n_chunks = ATTN_OUT_W // LANE
    perm_mats = [_residue_permutation(d) for _, d in ATTN_GROUPS if d > 1]
    return pl.pallas_call(
        _attn_kernel,
        grid=(B,),
        in_specs=[pl.BlockSpec((1, S, QKV_W), lambda b: (b, 0, 0)), _const_spec(hm.shape)]
        + [_const_spec(m.shape) for m in perm_mats],
        out_specs=pl.BlockSpec((1, S, ATTN_OUT_W), lambda b: (b, 0, 0)),
        out_shape=jax.ShapeDtypeStruct((B, S, ATTN_OUT_W), BF16),
        scratch_shapes=[
            pltpu.VMEM((S, ATTN_OUT_W), BF16),
            pltpu.VMEM((S, ATTN_OUT_W), BF16),
            pltpu.VMEM((S, ATTN_OUT_W), BF16),
            pltpu.VMEM((n_groups, n_chunks, S, LANE), F32),
            pltpu.VMEM((n_groups, n_chunks, S, LANE), F32),
            pltpu.VMEM((2, BLOCK, 2 * BLOCK), F32),
        ],
        compiler_params=pltpu.CompilerParams(
            dimension_semantics=("arbitrary",), vmem_limit_bytes=VMEM_LIMIT),
        name="attn",
    )(qkv, hm, *perm_mats)


def _ssm_kernel(u_ref, p_ref, pt_ref, wbu_ref, wc_ref, a_ref, d_ref, y_ref,
                bu_scr, x_scr, st_scr, us_scr):
    nb, tc, _ = u_ref.shape
    rows = nb * tc
    n_half = nb // SUBLANE
    hrows = SUBLANE * tc
    n_chunks = SSM_LANES // LANE
    tile_n = 2 * LANE
    assert SSM_PAIR * LANE == tile_n, "one scan pass covers the state lanes of one matmul tile"

    @pl.when(pl.program_id(0) == 0)
    def _():
        st_scr[...] = jnp.zeros_like(st_scr)
        bu_scr[...] = jnp.zeros_like(bu_scr)
        x_scr[...] = jnp.zeros_like(x_scr)
        us_scr[1] = jnp.zeros(us_scr.shape[1:], F32)
        us_scr[2] = jnp.zeros(us_scr.shape[1:], F32)

    us_scr[0] = us_scr[1]
    us_scr[1] = us_scr[2]

    def permute(pm, x):
        return jnp.concatenate(
            [jnp.dot(pm, x[h * hrows:(h + 1) * hrows], preferred_element_type=F32)
             for h in range(n_half)], axis=0)

    u_hi, u_mid, u_lo = _split3(u_ref[...].reshape(rows, SSM_W))
    p_fwd = p_ref[...]
    ut_hi = permute(p_fwd, u_hi)
    us_scr[2] = ut_hi + permute(p_fwd, u_mid) + permute(p_fwd, u_lo)
    ub = ut_hi.astype(BF16)

    out_tile = SSM_OUT_TILE
    k_chunks = (out_tile // SSM_CH * SSM_STATE) // LANE
    cp_per_out = k_chunks // SSM_PAIR
    ys = []

    for cp in range(n_chunks // SSM_PAIR):
        if cp % cp_per_out == 0:
            j = cp // cp_per_out
            acc = None
            for ri in range(2):
                xs = jnp.concatenate(
                    [x_scr[ri, j * k_chunks + c] for c in range(k_chunks)], axis=1)
                part = jnp.dot(xs, wc_ref[ri, j], preferred_element_type=F32)
                acc = part if acc is None else acc + part
            ys.append(acc)
        chains = [(cp * SSM_PAIR + j, h) for j in range(SSM_PAIR) for h in range(n_half)]
        a_re = {c: a_ref[0, c] for c, _ in chains}
        a_im = {c: a_ref[1, c] for c, _ in chains}
        carry = [st_scr[ri, c, h] for c, h in chains for ri in range(2)]
        pack = 4 // x_scr.dtype.itemsize
        for t0 in range(0, tc, pack):
            for j, (c, h) in enumerate(chains):
                xr, xi = carry[2 * j], carry[2 * j + 1]
                outs_r, outs_i = [], []
                for t in range(t0, t0 + pack):
                    row = pl.ds(h * hrows + SUBLANE * t, SUBLANE)
                    xr, xi = (a_re[c] * xr - a_im[c] * xi + bu_scr[0, c, row, :],
                              a_re[c] * xi + a_im[c] * xr + bu_scr[1, c, row, :])
                    outs_r.append(xr)
                    outs_i.append(xi)
                rows_p = pl.ds(h * hrows + SUBLANE * t0, SUBLANE * pack)
                x_scr[0, c, rows_p, :] = jnp.concatenate(outs_r, axis=0).astype(x_scr.dtype)
                x_scr[1, c, rows_p, :] = jnp.concatenate(outs_i, axis=0).astype(x_scr.dtype)
                carry[2 * j], carry[2 * j + 1] = xr, xi
        for j, (c, h) in enumerate(chains):
            st_scr[0, c, h] = carry[2 * j]
            st_scr[1, c, h] = carry[2 * j + 1]
        k0 = (cp * tile_n // SSM_STATE * SSM_CH) // LANE * LANE
        for ri in range(2):
            bu = jnp.dot(ub[:, k0:k0 + LANE], wbu_ref[ri, cp], preferred_element_type=F32)
            for c in range(SSM_PAIR):
                bu_scr[ri, cp * SSM_PAIR + c] = bu[:, c * LANE:(c + 1) * LANE]

    y = jnp.concatenate(ys, axis=1) + d_ref[...] * us_scr[0]
    y = jax.nn.gelu(y).astype(BF16)
    y = permute(pt_ref[...], y).astype(y_ref.dtype)
    y_ref[...] = y.reshape(nb, tc, SSM_W)


def _ssm_params(a_re, a_im, log_dt, b_re, b_im, c_re, c_im, d_skip):
    lr, li = a_re.astype(F32), a_im.astype(F32)
    dt = jnp.exp(log_dt.astype(F32))[:, None]
    mag = jnp.exp(lr * dt)
    ab_re, ab_im = mag * jnp.cos(li * dt), mag * jnp.sin(li * dt)
    den = lr * lr + li * li
    nr, ni = ab_re - 1.0, ab_im
    f_re = (nr * lr + ni * li) / den
    f_im = (ni * lr - nr * li) / den
    br, bi = b_re.astype(F32), b_im.astype(F32)
    bb_re = f_re[..., None] * br - f_im[..., None] * bi
    bb_im = f_re[..., None] * bi + f_im[..., None] * br
    tile_n = 2 * LANE
    n_bu = SSM_LANES // tile_n
    n_c = SSM_W // SSM_OUT_TILE
    k_rows = SSM_OUT_TILE // SSM_CH * SSM_STATE

    def bu_tiles(bb):
        rows = jnp.transpose(bb, (0, 2, 1)).reshape(SSM_W // LANE, LANE, SSM_STATE)
        k_slice = (np.arange(n_bu) * tile_n // SSM_STATE * SSM_CH) // LANE
        tiled = jnp.tile(rows[k_slice], (1, 1, tile_n // SSM_STATE))
        g_row = k_slice[:, None, None] * (LANE // SSM_CH) + np.arange(LANE)[None, :, None] // SSM_CH
        g_col = (np.arange(n_bu)[:, None, None] * tile_n + np.arange(tile_n)[None, None, :]) // SSM_STATE
        return jnp.where(g_row == g_col, tiled, 0.0)

    def c_tiles(c):
        rows = jnp.transpose(c, (0, 2, 1)).reshape(n_c, k_rows, SSM_CH)
        tiled = jnp.tile(rows, (1, 1, SSM_OUT_TILE // SSM_CH))
        same_group = ((np.arange(k_rows)[:, None] // SSM_STATE)
                      == (np.arange(SSM_OUT_TILE)[None, :] // SSM_CH))
        return jnp.where(same_group[None], tiled, 0.0)

    wbu = jnp.stack([bu_tiles(bb_re), bu_tiles(bb_im)]).astype(BF16)
    wc = jnp.stack([c_tiles(c_re.astype(F32)), c_tiles(-c_im.astype(F32))]).astype(BF16)
    n_chunks = SSM_LANES // LANE
    a = jnp.stack([ab_re.reshape(n_chunks, 1, LANE), ab_im.reshape(n_chunks, 1, LANE)])
    a = jnp.broadcast_to(a, (2, n_chunks, SUBLANE, LANE))
    d = d_skip.astype(F32).reshape(1, SSM_W)
    return wbu, wc, a, d


def _ssm_row_permutation(tc):
    dst = np.arange(SUBLANE * tc)
    src = (dst % SUBLANE) * tc + dst // SUBLANE
    p = src[:, None] == np.arange(SUBLANE * tc)[None, :]
    return jnp.asarray(p, dtype=BF16), jnp.asarray(p.T, dtype=BF16)


def _ssm(u, wbu, wc, a, d):
    B, S, _ = u.shape
    tc = SSM_TC
    n_chunks = SSM_LANES // LANE
    assert B % SUBLANE == 0 and S % tc == 0 and tc % 2 == 0
    p_fwd, p_bwd = _ssm_row_permutation(tc)
    n_steps = S // tc
    return pl.pallas_call(
        _ssm_kernel,
        grid=(n_steps + 2,),
        in_specs=[
            pl.BlockSpec((B, tc, SSM_W), lambda i: (0, jnp.minimum(i, n_steps - 1), 0)),
            _const_spec(p_fwd.shape),
            _const_spec(p_bwd.shape),
            _const_spec(wbu.shape),
            _const_spec(wc.shape),
            _const_spec(a.shape),
            _const_spec(d.shape),
        ],
        out_specs=pl.BlockSpec((B, tc, SSM_W), lambda i: (0, jnp.maximum(i - 2, 0), 0)),
        out_shape=jax.ShapeDtypeStruct((B, S, SSM_W), BF16),
        scratch_shapes=[
            pltpu.VMEM((2, n_chunks, B * tc, LANE), F32),
            pltpu.VMEM((2, n_chunks, B * tc, LANE), BF16),
            pltpu.VMEM((2, n_chunks, B // SUBLANE, SUBLANE, LANE), F32),
            pltpu.VMEM((3, B * tc, SSM_W), F32),
        ],
        compiler_params=pltpu.CompilerParams(
            dimension_semantics=("arbitrary",), vmem_limit_bytes=VMEM_LIMIT),
        name="ssm",
    )(u, p_fwd, p_bwd, wbu, wc, a, d)


def _mix_kernel(x_ref, attn_ref, y_ref, g_ref, wg_ref, wao_ref, wglu_ref, wout_ref, o_ref):
    tm = x_ref.shape[1]
    sub = tm // MIX_SPLIT
    for r0 in range(0, tm, sub):
        rs = slice(r0, r0 + sub)
        x = x_ref[0, rs, :]
        h = _rmsnorm(x, g_ref[...]).astype(BF16)
        gate = jax.nn.sigmoid(jnp.dot(h, wg_ref[:, QKVU_W:], preferred_element_type=F32))
        attn_d = jnp.dot(attn_ref[0, rs, :], wao_ref[...], preferred_element_type=F32)
        z = jnp.dot(y_ref[0, rs, :], wglu_ref[...], preferred_element_type=F32)
        ssm_out = z[:, :D_MODEL] * jax.nn.sigmoid(z[:, D_MODEL:])
        merged = gate[:, :D_MODEL] * attn_d + gate[:, D_MODEL:] * ssm_out
        o_ref[0, rs, :] = x + jnp.dot(merged.astype(BF16), wout_ref[...], preferred_element_type=F32)


def _mix(x, attn, y, g, wg, wao, wglu, wout):
    B, S, D = x.shape
    tm = TM_MIX
    tok = lambda w: pl.BlockSpec((1, tm, w), lambda b, i: (b, i, 0))
    return pl.pallas_call(
        _mix_kernel,
        grid=(B, S // tm),
        in_specs=[tok(D), tok(ATTN_OUT_W), tok(SSM_W), _const_spec(g.shape), _const_spec(wg.shape),
                  _const_spec(wao.shape), _const_spec(wglu.shape), _const_spec(wout.shape)],
        out_specs=tok(D),
        out_shape=jax.ShapeDtypeStruct((B, S, D), F32),
        compiler_params=pltpu.CompilerParams(
            dimension_semantics=("arbitrary", "arbitrary"), vmem_limit_bytes=VMEM_LIMIT),
        name="mix",
    )(x, attn, y, g, wg, wao, wglu, wout)


def _ffn_kernel(x_ref, g2_ref, gf_ref, wgate_ref, wup_ref, wdown_ref, o_ref):
    tm = x_ref.shape[1]
    sub = tm // FFN_SPLIT
    for r0 in range(0, tm, sub):
        rs = slice(r0, r0 + sub)
        x = x_ref[0, rs, :]
        h = _rmsnorm(x, g2_ref[...]).astype(BF16)
        gate = jnp.dot(h, wgate_ref[...], preferred_element_type=F32)
        up = jnp.dot(h, wup_ref[...], preferred_element_type=F32)
        act = (jax.nn.silu(gate) * up).astype(BF16)
        ff = jnp.dot(act, wdown_ref[...], preferred_element_type=F32)
        o_ref[0, rs, :] = _rmsnorm(x + ff, gf_ref[...])


def _ffn(x, g2, gf, wgate, wup, wdown):
    B, S, D = x.shape
    tm = TM_FFN
    tok = pl.BlockSpec((1, tm, D), lambda b, i: (b, i, 0))
    return pl.pallas_call(
        _ffn_kernel,
        grid=(B, S // tm),
        in_specs=[tok, _const_spec(g2.shape), _const_spec(gf.shape), _const_spec(wgate.shape),
                  _const_spec(wup.shape), _const_spec(wdown.shape)],
        out_specs=tok,
        out_shape=jax.ShapeDtypeStruct((B, S, D), F32),
        compiler_params=pltpu.CompilerParams(
            dimension_semantics=("arbitrary", "arbitrary"), vmem_limit_bytes=VMEM_LIMIT),
        name="ffn",
    )(x, g2, gf, wgate, wup, wdown)


def kernel(x, norm_mix_g, w_in, ssm_a_re, ssm_a_im, ssm_log_dt, ssm_b_re, ssm_b_im, ssm_c_re, ssm_c_im, ssm_d, w_glu, w_attn_out, w_out, norm_ffn_g, w_ffn_gate, w_ffn_up, w_ffn_down, norm_final_g):
    B, S, D = x.shape
    assert D == D_MODEL and norm_mix_g.shape[0] == 1, "single-layer block expected"
    layer = 0
    g_mix = norm_mix_g[layer].reshape(1, D).astype(F32)
    assert w_in.shape[2] == QKVU_W + 2 * D
    w_in_b = w_in[layer].astype(BF16)

    qkv, u = _inproj(x, g_mix, w_in_b, _rope_tables(S))
    attn = _attn(qkv)
    wbu, wc, a, d = _ssm_params(
        ssm_a_re[layer], ssm_a_im[layer], ssm_log_dt[layer], ssm_b_re[layer], ssm_b_im[layer],
        ssm_c_re[layer], ssm_c_im[layer], ssm_d[layer])
    y = _ssm(u, wbu, wc, a, d)
    x1 = _mix(x, attn, y, g_mix, w_in_b, w_attn_out[layer].astype(BF16),
              w_glu[layer].astype(BF16), w_out[layer].astype(BF16))
    return _ffn(x1, norm_ffn_g[layer].reshape(1, D).astype(F32),
                norm_final_g.reshape(1, D).astype(F32), w_ffn_gate[layer].astype(BF16),
                w_ffn_up[layer].astype(BF16), w_ffn_down[layer].astype(BF16))
```

```python
import functools

import jax
import jax.numpy as jnp
import numpy as np
from jax import lax
from jax.experimental import pallas as pl
from jax.experimental.pallas import tpu as pltpu

F32 = jnp.float32
BF16 = jnp.bfloat16

D_MODEL = 1024
HEAD_DIM = 64
HEADS_PER_GROUP = 4
ATTN_GROUPS = ((128, 1), (512, 4), (2048, 16))
N_ATTN_HEADS = HEADS_PER_GROUP * len(ATTN_GROUPS)
ATTN_OUT_W = HEADS_PER_GROUP * HEAD_DIM
ROPE_DIM = HEAD_DIM // 4
ROPE_THETA = 500000.0
BLOCK = 128
SSM_CH = 16
SSM_GROUPS = 32
SSM_W = SSM_CH * SSM_GROUPS
SSM_STATE = 64
SSM_LANES = SSM_GROUPS * SSM_STATE
D_FF = 2816
QK_W = 2 * N_ATTN_HEADS * HEAD_DIM
QKV_W = 3 * N_ATTN_HEADS * HEAD_DIM
QKVU_W = QKV_W + SSM_W
RMS_EPS = 1e-6
NEG_INF = -1e30

LANE = 128
PERM_ROWS = 256
VMEM_LIMIT = 56 * 1024 * 1024

TS_INPROJ = 2048
TM_MIX = 2048
TM_FFN = 1024
FFN_SPLIT = 4
MIX_SPLIT = 8
INPROJ_SPLIT = 8
SSM_TC = 32
ATTN_UNROLL = 16
SSM_OUT_TILE = 2 * LANE
SSM_PAIR = 2
SUBLANE = 8


def _rmsnorm(x, g):
    return x * lax.rsqrt(jnp.mean(x * x, axis=-1, keepdims=True) + RMS_EPS) * g


def _const_spec(shape):
    return pl.BlockSpec(shape, lambda *_: (0,) * len(shape), pipeline_mode=pl.Buffered(1))


def _inproj_kernel(x_ref, g_ref, w_ref, rc_ref, ra_ref, rb_ref, qkv_ref, u_ref):
    ts = x_ref.shape[1]
    sub = ts // INPROJ_SPLIT
    half = ROPE_DIM // 2
    for r0 in range(0, ts, sub):
        rs = slice(r0, r0 + sub)
        h = _rmsnorm(x_ref[0, rs, :], g_ref[...]).astype(BF16)
        p = jnp.dot(h, w_ref[...], preferred_element_type=F32)
        rc, ra, rb = rc_ref[rs, :], ra_ref[rs, :], rb_ref[rs, :]
        for c in range(QK_W // LANE):
            t = p[:, c * LANE:(c + 1) * LANE]
            t = t * rc + pltpu.roll(t, LANE - half, 1) * ra + pltpu.roll(t, half, 1) * rb
            qkv_ref[0, rs, c * LANE:(c + 1) * LANE] = t.astype(BF16)
        qkv_ref[0, rs, QK_W:QKV_W] = p[:, QK_W:QKV_W].astype(BF16)
        u_ref[0, rs, :] = p[:, QKV_W:QKVU_W]


def _rope_tables(S):
    half = ROPE_DIM // 2
    pos = np.arange(S, dtype=np.float64)
    inv = np.power(np.float64(ROPE_THETA), -np.arange(half, dtype=np.float64) * 2.0 / ROPE_DIM)
    ang = pos[:, None] * inv[None, :]
    cos, sin = np.cos(ang), np.sin(ang)
    ones = np.ones((S, HEAD_DIM - ROPE_DIM))
    zeros_h = np.zeros((S, half))
    zeros_r = np.zeros((S, HEAD_DIM - ROPE_DIM))
    per_head_c = np.concatenate([cos, cos, ones], axis=1)
    per_head_a = np.concatenate([-sin, zeros_h, zeros_r], axis=1)
    per_head_b = np.concatenate([zeros_h, sin, zeros_r], axis=1)
    rep = LANE // HEAD_DIM
    return tuple(jnp.asarray(np.tile(t, (1, rep)), dtype=F32)
                 for t in (per_head_c, per_head_a, per_head_b))


def _inproj(x, g, w, rope):
    B, S, D = x.shape
    ts = TS_INPROJ
    tab_spec = pl.BlockSpec((ts, LANE), lambda b, i: (i, 0))
    return pl.pallas_call(
        _inproj_kernel,
        grid=(B, S // ts),
        in_specs=[
            pl.BlockSpec((1, ts, D), lambda b, i: (b, i, 0)),
            _const_spec((1, D)),
            pl.BlockSpec((D, QKVU_W), lambda b, i: (0, 0), pipeline_mode=pl.Buffered(1)),
            tab_spec, tab_spec, tab_spec,
        ],
        out_specs=[
            pl.BlockSpec((1, ts, QKV_W), lambda b, i: (b, i, 0)),
            pl.BlockSpec((1, ts, SSM_W), lambda b, i: (b, i, 0)),
        ],
        out_shape=[
            jax.ShapeDtypeStruct((B, S, QKV_W), BF16),
            jax.ShapeDtypeStruct((B, S, SSM_W), F32),
        ],
        compiler_params=pltpu.CompilerParams(
            dimension_semantics=("arbitrary", "arbitrary"), vmem_limit_bytes=VMEM_LIMIT),
        name="inproj",
    )(x, g, w, *rope)


def _split3(x):
    hi = x.astype(BF16)
    r1 = x - hi.astype(F32)
    mid = r1.astype(BF16)
    lo = (r1 - mid.astype(F32)).astype(BF16)
    return hi, mid, lo


def _head_combine(per_head, lane_lo):
    return jnp.concatenate(
        [jnp.where(lane_lo, per_head[0], per_head[1]), jnp.where(lane_lo, per_head[2], per_head[3])],
        axis=1)


def _attn_block(q, k, v, bias, hm, lane_lo):
    nh = HEADS_PER_GROUP
    qm = jnp.concatenate([q * hm[h:h + 1, :] for h in range(nh)], axis=0)
    s = lax.dot_general(qm, k, (((1,), (1,)), ((), ())), preferred_element_type=F32)
    pv_h, m_h, l_h = [], [], []
    for h in range(nh):
        sh = s[h * BLOCK:(h + 1) * BLOCK] + bias
        m = jnp.max(sh, axis=-1, keepdims=True)
        p = jnp.exp(sh - m)
        l = jnp.sum(p, axis=-1, keepdims=True)
        pv = jnp.dot(p.astype(BF16), v, preferred_element_type=F32)
        pv_h.append(pv[:, (h // 2) * LANE:(h // 2 + 1) * LANE])
        m_h.append(jnp.broadcast_to(m, (BLOCK, LANE)))
        l_h.append(jnp.broadcast_to(l, (BLOCK, LANE)))
    l_b = _head_combine(l_h, lane_lo)
    o = _head_combine(pv_h, lane_lo) * (1.0 / l_b)
    return o, _head_combine(m_h, lane_lo) + jnp.log(l_b)


def _attn_kernel(qkv_ref, hm_ref, pf4_ref, pf16_ref, o_ref, qs, ks, vs, o_scr, l_scr, bias_scr):
    S = qkv_ref.shape[1]
    n_chunks = ATTN_OUT_W // LANE
    perms = {4: pf4_ref, 16: pf16_ref}
    lane_lo = lax.broadcasted_iota(jnp.int32, (BLOCK, LANE), 1) < HEAD_DIM
    qi = lax.broadcasted_iota(jnp.int32, (BLOCK, 2 * BLOCK), 0)
    ki = lax.broadcasted_iota(jnp.int32, (BLOCK, 2 * BLOCK), 1)
    dist = qi + BLOCK - ki
    band = (dist >= 0) & (dist <= BLOCK)
    bias_scr[1] = jnp.where(band, 0.0, NEG_INF)
    bias_scr[0] = jnp.where(band & (ki >= BLOCK), 0.0, NEG_INF)
    hm = hm_ref[...]

    for gi, (window, d) in enumerate(ATTN_GROUPS):
        L = S // d
        nb = L // BLOCK
        cols = [part * N_ATTN_HEADS * HEAD_DIM + gi * ATTN_OUT_W for part in range(3)]
        plen = PERM_ROWS // d
        ppb = BLOCK // plen
        if d > 1:
            pf = perms[d][...]
            for scr, col in zip((qs, ks, vs), cols):
                for mblk in range(S // PERM_ROWS):
                    rs = slice(mblk * PERM_ROWS, (mblk + 1) * PERM_ROWS)
                    blk = qkv_ref[0, rs, col:col + ATTN_OUT_W]
                    scr[rs, :] = jnp.dot(pf, blk, preferred_element_type=F32).astype(BF16)

        def piece_rows(r, n, jj, d=d, plen=plen, ppb=ppb):
            if d == 1:
                return pl.ds(pl.multiple_of(n * BLOCK, BLOCK), BLOCK)
            return pl.ds(pl.multiple_of((n * ppb + jj) * PERM_ROWS + r * plen, plen), plen)

        def load(scr, col, r, n, d=d, ppb=ppb, piece_rows=piece_rows):
            if d == 1:
                return qkv_ref[0, piece_rows(r, n, 0), col:col + ATTN_OUT_W]
            return jnp.concatenate([scr[piece_rows(r, n, jj), :] for jj in range(ppb)], axis=0)

        def body(idx, carry, gi=gi, d=d, nb=nb, cols=cols, load=load, plen=plen, ppb=ppb,
                 piece_rows=piece_rows):
            r = idx // nb
            n = idx % nb
            q = load(qs, cols[0], r, n)
            k = load(ks, cols[1], r, n)
            v = load(vs, cols[2], r, n)
            if nb > 1:
                prev = jnp.maximum(n - 1, 0)
                k = jnp.concatenate([load(ks, cols[1], r, prev), k], axis=0)
                v = jnp.concatenate([load(vs, cols[2], r, prev), v], axis=0)
                bias = bias_scr[jnp.minimum(n, 1)]
            else:
                bias = bias_scr[1, :, BLOCK:]
            o, lse_b = _attn_block(q, k, v, bias, hm, lane_lo)
            start = r + d * BLOCK * n
            rows = pl.ds(pl.multiple_of(start, BLOCK), BLOCK) if d == 1 else pl.ds(start, BLOCK, stride=d)
            for c in range(n_chunks):
                o_scr[gi, c, rows, :] = o[:, c * LANE:(c + 1) * LANE]
                l_scr[gi, c, rows, :] = lse_b[:, c * LANE:(c + 1) * LANE]
            return carry

        lax.fori_loop(0, d * nb, body, 0, unroll=ATTN_UNROLL)

    n_groups = len(ATTN_GROUPS)

    def merge(i, carry):
        rs = pl.ds(pl.multiple_of(i * PERM_ROWS, PERM_ROWS), PERM_ROWS)
        for c in range(n_chunks):
            ls = [l_scr[g, c, rs, :] for g in range(n_groups)]
            m = functools.reduce(jnp.maximum, ls)
            es = [jnp.exp(l - m) for l in ls]
            inv_den = 1.0 / functools.reduce(lambda a, b: a + b, es)
            acc = None
            for g in range(n_groups):
                term = (es[g] * inv_den) * o_scr[g, c, rs, :]
                acc = term if acc is None else acc + term
            o_ref[0, rs, c * LANE:(c + 1) * LANE] = acc.astype(o_ref.dtype)
        return carry

    lax.fori_loop(0, S // PERM_ROWS, merge, 0)


def _residue_permutation(d):
    dst = np.arange(PERM_ROWS)
    plen = PERM_ROWS // d
    src = (dst % plen) * d + dst // plen
    return jnp.asarray(src[:, None] == np.arange(PERM_ROWS)[None, :], dtype=BF16)


def _attn(qkv):
    B, S, _ = qkv.shape
    n_groups = len(ATTN_GROUPS)
    for window, d in ATTN_GROUPS:
        assert window // d == BLOCK and S % (d * BLOCK) == 0
        assert d == 1 or (PERM_ROWS % d == 0 and BLOCK % (PERM_ROWS // d) == 0 and S % PERM_ROWS == 0)
    lane_head = np.arange(ATTN_OUT_W) // HEAD_DIM
    head = np.arange(HEADS_PER_GROUP)
    hm = jnp.asarray(np.where(head[:, None] == lane_head[None, :], HEAD_DIM ** -0.5, 0.0), dtype=BF16)
    n_chunks = ATTN_OUT_W // LANE
    perm_mats = [_residue_permutation(d) for _, d in ATTN_GROUPS if d > 1]
    return pl.pallas_call(
        _attn_kernel,
        grid=(B,),
        in_specs=[pl.BlockSpec((1, S, QKV_W), lambda b: (b, 0, 0)), _const_spec(hm.shape)]
        + [_const_spec(m.shape) for m in perm_mats],
        out_specs=pl.BlockSpec((1, S, ATTN_OUT_W), lambda b: (b, 0, 0)),
        out_shape=jax.ShapeDtypeStruct((B, S, ATTN_OUT_W), BF16),
        scratch_shapes=[
            pltpu.VMEM((S, ATTN_OUT_W), BF16),
            pltpu.VMEM((S, ATTN_OUT_W), BF16),
            pltpu.VMEM((S, ATTN_OUT_W), BF16),
            pltpu.VMEM((n_groups, n_chunks, S, LANE), F32),
            pltpu.VMEM((n_groups, n_chunks, S, LANE), F32),
            pltpu.VMEM((2, BLOCK, 2 * BLOCK), F32),
        ],
        compiler_params=pltpu.CompilerParams(
            dimension_semantics=("arbitrary",), vmem_limit_bytes=VMEM_LIMIT),
        name="attn",
    )(qkv, hm, *perm_mats)


def _ssm_kernel(u_ref, p_ref, pt_ref, wbu_ref, wc_ref, a_ref, d_ref, y_ref,
                bu_scr, x_scr, st_scr, us_scr):
    nb, tc, _ = u_ref.shape
    rows = nb * tc
    n_half = nb // SUBLANE
    hrows = SUBLANE * tc
    n_chunks = SSM_LANES // LANE
    tile_n = 2 * LANE
    assert SSM_PAIR * LANE == tile_n, "one scan pass covers the state lanes of one matmul tile"

    @pl.when(pl.program_id(0) == 0)
    def _():
        st_scr[...] = jnp.zeros_like(st_scr)
        bu_scr[...] = jnp.zeros_like(bu_scr)
        x_scr[...] = jnp.zeros_like(x_scr)
        us_scr[1] = jnp.zeros(us_scr.shape[1:], F32)
        us_scr[2] = jnp.zeros(us_scr.shape[1:], F32)

    us_scr[0] = us_scr[1]
    us_scr[1] = us_scr[2]

    def permute(pm, x):
        return jnp.concatenate(
            [jnp.dot(pm, x[h * hrows:(h + 1) * hrows], preferred_element_type=F32)
             for h in range(n_half)], axis=0)

    u_hi, u_mid, u_lo = _split3(u_ref[...].reshape(rows, SSM_W))
    p_fwd = p_ref[...]
    ut_hi = permute(p_fwd, u_hi)
    us_scr[2] = ut_hi + permute(p_fwd, u_mid) + permute(p_fwd, u_lo)
    ub = ut_hi.astype(BF16)

    out_tile = SSM_OUT_TILE
    k_chunks = (out_tile // SSM_CH * SSM_STATE) // LANE
    cp_per_out = k_chunks // SSM_PAIR
    ys = []

    for cp in range(n_chunks // SSM_PAIR):
        if cp % cp_per_out == 0:
            j = cp // cp_per_out
            acc = None
            for ri in range(2):
                xs = jnp.concatenate(
                    [x_scr[ri, j * k_chunks + c] for c in range(k_chunks)], axis=1)
                part = jnp.dot(xs, wc_ref[ri, j], preferred_element_type=F32)
                acc = part if acc is None else acc + part
            ys.append(acc)
        chains = [(cp * SSM_PAIR + j, h) for j in range(SSM_PAIR) for h in range(n_half)]
        a_re = {c: a_ref[0, c] for c, _ in chains}
        a_im = {c: a_ref[1, c] for c, _ in chains}
        carry = [st_scr[ri, c, h] for c, h in chains for ri in range(2)]
        pack = 4 // x_scr.dtype.itemsize
        for t0 in range(0, tc, pack):
            for j, (c, h) in enumerate(chains):
                xr, xi = carry[2 * j], carry[2 * j + 1]
                outs_r, outs_i = [], []
                for t in range(t0, t0 + pack):
                    row = pl.ds(h * hrows + SUBLANE * t, SUBLANE)
                    xr, xi = (a_re[c] * xr - a_im[c] * xi + bu_scr[0, c, row, :],
                              a_re[c] * xi + a_im[c] * xr + bu_scr[1, c, row, :])
                    outs_r.append(xr)
                    outs_i.append(xi)
                rows_p = pl.ds(h * hrows + SUBLANE * t0, SUBLANE * pack)
                x_scr[0, c, rows_p, :] = jnp.concatenate(outs_r, axis=0).astype(x_scr.dtype)
                x_scr[1, c, rows_p, :] = jnp.concatenate(outs_i, axis=0).astype(x_scr.dtype)
                carry[2 * j], carry[2 * j + 1] = xr, xi
        for j, (c, h) in enumerate(chains):
            st_scr[0, c, h] = carry[2 * j]
            st_scr[1, c, h] = carry[2 * j + 1]
        k0 = (cp * tile_n // SSM_STATE * SSM_CH) // LANE * LANE
        for ri in range(2):
            bu = jnp.dot(ub[:, k0:k0 + LANE], wbu_ref[ri, cp], preferred_element_type=F32)
            for c in range(SSM_PAIR):
                bu_scr[ri, cp * SSM_PAIR + c] = bu[:, c * LANE:(c + 1) * LANE]

    y = jnp.concatenate(ys, axis=1) + d_ref[...] * us_scr[0]
    y = jax.nn.gelu(y).astype(BF16)
    y = permute(pt_ref[...], y).astype(y_ref.dtype)
    y_ref[...] = y.reshape(nb, tc, SSM_W)


def _ssm_params(a_re, a_im, log_dt, b_re, b_im, c_re, c_im, d_skip):
    lr, li = a_re.astype(F32), a_im.astype(F32)
    dt = jnp.exp(log_dt.astype(F32))[:, None]
    mag = jnp.exp(lr * dt)
    ab_re, ab_im = mag * jnp.cos(li * dt), mag * jnp.sin(li * dt)
    den = lr * lr + li * li
    nr, ni = ab_re - 1.0, ab_im
    f_re = (nr * lr + ni * li) / den
    f_im = (ni * lr - nr * li) / den
    br, bi = b_re.astype(F32), b_im.astype(F32)
    bb_re = f_re[..., None] * br - f_im[..., None] * bi
    bb_im = f_re[..., None] * bi + f_im[..., None] * br
    tile_n = 2 * LANE
    n_bu = SSM_LANES // tile_n
    n_c = SSM_W // SSM_OUT_TILE
    k_rows = SSM_OUT_TILE // SSM_CH * SSM_STATE

    def bu_tiles(bb):
        rows = jnp.transpose(bb, (0, 2, 1)).reshape(SSM_W // LANE, LANE, SSM_STATE)
        k_slice = (np.arange(n_bu) * tile_n // SSM_STATE * SSM_CH) // LANE
        tiled = jnp.tile(rows[k_slice], (1, 1, tile_n // SSM_STATE))
        g_row = k_slice[:, None, None] * (LANE // SSM_CH) + np.arange(LANE)[None, :, None] // SSM_CH
        g_col = (np.arange(n_bu)[:, None, None] * tile_n + np.arange(tile_n)[None, None, :]) // SSM_STATE
        return jnp.where(g_row == g_col, tiled, 0.0)

    def c_tiles(c):
        rows = jnp.transpose(c, (0, 2, 1)).reshape(n_c, k_rows, SSM_CH)
        tiled = jnp.tile(rows, (1, 1, SSM_OUT_TILE // SSM_CH))
        same_group = ((np.arange(k_rows)[:, None] // SSM_STATE)
                      == (np.arange(SSM_OUT_TILE)[None, :] // SSM_CH))
        return jnp.where(same_group[None], tiled, 0.0)

    wbu = jnp.stack([bu_tiles(bb_re), bu_tiles(bb_im)]).astype(BF16)
    wc = jnp.stack([c_tiles(c_re.astype(F32)), c_tiles(-c_im.astype(F32))]).astype(BF16)
    n_chunks = SSM_LANES // LANE
    a = jnp.stack([ab_re.reshape(n_chunks, 1, LANE), ab_im.reshape(n_chunks, 1, LANE)])
    a = jnp.broadcast_to(a, (2, n_chunks, SUBLANE, LANE))
    d = d_skip.astype(F32).reshape(1, SSM_W)
    return wbu, wc, a, d


def _ssm_row_permutation(tc):
    dst = np.arange(SUBLANE * tc)
    src = (dst % SUBLANE) * tc + dst // SUBLANE
    p = src[:, None] == np.arange(SUBLANE * tc)[None, :]
    return jnp.asarray(p, dtype=BF16), jnp.asarray(p.T, dtype=BF16)


def _ssm(u, wbu, wc, a, d):
    B, S, _ = u.shape
    tc = SSM_TC
    n_chunks = SSM_LANES // LANE
    assert B % SUBLANE == 0 and S % tc == 0 and tc % 2 == 0
    p_fwd, p_bwd = _ssm_row_permutation(tc)
    n_steps = S // tc
    return pl.pallas_call(
        _ssm_kernel,
        grid=(n_steps + 2,),
        in_specs=[
            pl.BlockSpec((B, tc, SSM_W), lambda i: (0, jnp.minimum(i, n_steps - 1), 0)),
            _const_spec(p_fwd.shape),
            _const_spec(p_bwd.shape),
            _const_spec(wbu.shape),
            _const_spec(wc.shape),
            _const_spec(a.shape),
            _const_spec(d.shape),
        ],
        out_specs=pl.BlockSpec((B, tc, SSM_W), lambda i: (0, jnp.maximum(i - 2, 0), 0)),
        out_shape=jax.ShapeDtypeStruct((B, S, SSM_W), BF16),
        scratch_shapes=[
            pltpu.VMEM((2, n_chunks, B * tc, LANE), F32),
            pltpu.VMEM((2, n_chunks, B * tc, LANE), BF16),
            pltpu.VMEM((2, n_chunks, B // SUBLANE, SUBLANE, LANE), F32),
            pltpu.VMEM((3, B * tc, SSM_W), F32),
        ],
        compiler_params=pltpu.CompilerParams(
            dimension_semantics=("arbitrary",), vmem_limit_bytes=VMEM_LIMIT),
        name="ssm",
    )(u, p_fwd, p_bwd, wbu, wc, a, d)


def _mix_kernel(x_ref, attn_ref, y_ref, g_ref, wg_ref, wao_ref, wglu_ref, wout_ref, o_ref):
    tm = x_ref.shape[1]
    sub = tm // MIX_SPLIT
    for r0 in range(0, tm, sub):
        rs = slice(r0, r0 + sub)
        x = x_ref[0, rs, :]
        h = _rmsnorm(x, g_ref[...]).astype(BF16)
        gate = jax.nn.sigmoid(jnp.dot(h, wg_ref[:, QKVU_W:], preferred_element_type=F32))
        attn_d = jnp.dot(attn_ref[0, rs, :], wao_ref[...], preferred_element_type=F32)
        z = jnp.dot(y_ref[0, rs, :], wglu_ref[...], preferred_element_type=F32)
        ssm_out = z[:, :D_MODEL] * jax.nn.sigmoid(z[:, D_MODEL:])
        merged = gate[:, :D_MODEL] * attn_d + gate[:, D_MODEL:] * ssm_out
        o_ref[0, rs, :] = x + jnp.dot(merged.astype(BF16), wout_ref[...], preferred_element_type=F32)


def _mix(x, attn, y, g, wg, wao, wglu, wout):
    B, S, D = x.shape
    tm = TM_MIX
    tok = lambda w: pl.BlockSpec((1, tm, w), lambda b, i: (b, i, 0))
    return pl.pallas_call(
        _mix_kernel,
        grid=(B, S // tm),
        in_specs=[tok(D), tok(ATTN_OUT_W), tok(SSM_W), _const_spec(g.shape), _const_spec(wg.shape),
                  _const_spec(wao.shape), _const_spec(wglu.shape), _const_spec(wout.shape)],
        out_specs=tok(D),
        out_shape=jax.ShapeDtypeStruct((B, S, D), F32),
        compiler_params=pltpu.CompilerParams(
            dimension_semantics=("arbitrary", "arbitrary"), vmem_limit_bytes=VMEM_LIMIT),
        name="mix",
    )(x, attn, y, g, wg, wao, wglu, wout)


def _ffn_kernel(x_ref, g2_ref, gf_ref, wgate_ref, wup_ref, wdown_ref, o_ref):
    tm = x_ref.shape[1]
    sub = tm // FFN_SPLIT
    for r0 in range(0, tm, sub):
        rs = slice(r0, r0 + sub)
        x = x_ref[0, rs, :]
        h = _rmsnorm(x, g2_ref[...]).astype(BF16)
        gate = jnp.dot(h, wgate_ref[...], preferred_element_type=F32)
        up = jnp.dot(h, wup_ref[...], preferred_element_type=F32)
        act = (jax.nn.silu(gate) * up).astype(BF16)
        ff = jnp.dot(act, wdown_ref[...], preferred_element_type=F32)
        o_ref[0, rs, :] = _rmsnorm(x + ff, gf_ref[...])


def _ffn(x, g2, gf, wgate, wup, wdown):
    B, S, D = x.shape
    tm = TM_FFN
    tok = pl.BlockSpec((1, tm, D), lambda b, i: (b, i, 0))
    return pl.pallas_call(
        _ffn_kernel,
        grid=(B, S // tm),
        in_specs=[tok, _const_spec(g2.shape), _const_spec(gf.shape), _const_spec(wgate.shape),
                  _const_spec(wup.shape), _const_spec(wdown.shape)],
        out_specs=tok,
        out_shape=jax.ShapeDtypeStruct((B, S, D), F32),
        compiler_params=pltpu.CompilerParams(
            dimension_semantics=("arbitrary", "arbitrary"), vmem_limit_bytes=VMEM_LIMIT),
        name="ffn",
    )(x, g2, gf, wgate, wup, wdown)


def kernel(x, norm_mix_g, w_in, ssm_a_re, ssm_a_im, ssm_log_dt, ssm_b_re, ssm_b_im, ssm_c_re, ssm_c_im, ssm_d, w_glu, w_attn_out, w_out, norm_ffn_g, w_ffn_gate, w_ffn_up, w_ffn_down, norm_final_g):
    B, S, D = x.shape
    assert D == D_MODEL and norm_mix_g.shape[0] == 1, "single-layer block expected"
    layer = 0
    g_mix = norm_mix_g[layer].reshape(1, D).astype(F32)
    assert w_in.shape[2] == QKVU_W + 2 * D
    w_in_b = w_in[layer].astype(BF16)

    qkv, u = _inproj(x, g_mix, w_in_b, _rope_tables(S))
    attn = _attn(qkv)
    wbu, wc, a, d = _ssm_params(
        ssm_a_re[layer], ssm_a_im[layer], ssm_log_dt[layer], ssm_b_re[layer], ssm_b_im[layer],
        ssm_c_re[layer], ssm_c_im[layer], ssm_d[layer])
    y = _ssm(u, wbu, wc, a, d)
    x1 = _mix(x, attn, y, g_mix, w_in_b, w_attn_out[layer].astype(BF16),
              w_glu[layer].astype(BF16), w_out[layer].astype(BF16))
    return _ffn(x1, norm_ffn_g[layer].reshape(1, D).astype(F32),
                norm_final_g.reshape(1, D).astype(F32), w_ffn_gate[layer].astype(BF16),
                w_ffn_up[layer].astype(BF16), w_ffn_down[layer].astype(BF16))
```

```python
import functools

import jax
import jax.numpy as jnp
import numpy as np
from jax import lax
from jax.experimental import pallas as pl
from jax.experimental.pallas import tpu as pltpu

F32 = jnp.float32
BF16 = jnp.bfloat16

D_MODEL = 1024
HEAD_DIM = 64
HEADS_PER_GROUP = 4
ATTN_GROUPS = ((128, 1), (512, 4), (2048, 16))
N_ATTN_HEADS = HEADS_PER_GROUP * len(ATTN_GROUPS)
ATTN_OUT_W = HEADS_PER_GROUP * HEAD_DIM
ROPE_DIM = HEAD_DIM // 4
ROPE_THETA = 500000.0
BLOCK = 128
SSM_CH = 16
SSM_GROUPS = 32
SSM_W = SSM_CH * SSM_GROUPS
SSM_STATE = 64
SSM_LANES = SSM_GROUPS * SSM_STATE
D_FF = 2816
QK_W = 2 * N_ATTN_HEADS * HEAD_DIM
QKV_W = 3 * N_ATTN_HEADS * HEAD_DIM
QKVU_W = QKV_W + SSM_W
RMS_EPS = 1e-6
NEG_INF = -1e30

LANE = 128
PERM_ROWS = 256
VMEM_LIMIT = 56 * 1024 * 1024

TS_INPROJ = 2048
TM_MIX = 2048
TM_FFN = 1024
TM_MIXFFN = 1024
MIXFFN_SPLIT = 4
FFN_SPLIT = 4
MIX_SPLIT = 8
INPROJ_SPLIT = 8
SSM_TC = 32
ATTN_UNROLL = 16
SSM_OUT_TILE = 2 * LANE
SSM_PAIR = 2
SUBLANE = 8


def _rmsnorm(x, g):
    return x * lax.rsqrt(jnp.mean(x * x, axis=-1, keepdims=True) + RMS_EPS) * g


def _const_spec(shape):
    return pl.BlockSpec(shape, lambda *_: (0,) * len(shape), pipeline_mode=pl.Buffered(1))


def _inproj_kernel(x_ref, g_ref, w_ref, rc_ref, ra_ref, rb_ref, qkv_ref, u_ref):
    ts = x_ref.shape[1]
    sub = ts // INPROJ_SPLIT
    half = ROPE_DIM // 2
    for r0 in range(0, ts, sub):
        rs = slice(r0, r0 + sub)
        h = _rmsnorm(x_ref[0, rs, :], g_ref[...]).astype(BF16)
        p = jnp.dot(h, w_ref[...], preferred_element_type=F32)
        rc, ra, rb = rc_ref[rs, :], ra_ref[rs, :], rb_ref[rs, :]
        for c in range(QK_W // LANE):
            t = p[:, c * LANE:(c + 1) * LANE]
            t = t * rc + pltpu.roll(t, LANE - half, 1) * ra + pltpu.roll(t, half, 1) * rb
            qkv_ref[0, rs, c * LANE:(c + 1) * LANE] = t.astype(BF16)
        qkv_ref[0, rs, QK_W:QKV_W] = p[:, QK_W:QKV_W].astype(BF16)
        u_ref[0, rs, :] = p[:, QKV_W:QKVU_W]


def _rope_tables(S):
    half = ROPE_DIM // 2
    pos = np.arange(S, dtype=np.float64)
    inv = np.power(np.float64(ROPE_THETA), -np.arange(half, dtype=np.float64) * 2.0 / ROPE_DIM)
    ang = pos[:, None] * inv[None, :]
    cos, sin = np.cos(ang), np.sin(ang)
    ones = np.ones((S, HEAD_DIM - ROPE_DIM))
    zeros_h = np.zeros((S, half))
    zeros_r = np.zeros((S, HEAD_DIM - ROPE_DIM))
    per_head_c = np.concatenate([cos, cos, ones], axis=1)
    per_head_a = np.concatenate([-sin, zeros_h, zeros_r], axis=1)
    per_head_b = np.concatenate([zeros_h, sin, zeros_r], axis=1)
    rep = LANE // HEAD_DIM
    return tuple(jnp.asarray(np.tile(t, (1, rep)), dtype=F32)
                 for t in (per_head_c, per_head_a, per_head_b))


def _inproj(x, g, w, rope):
    B, S, D = x.shape
    ts = TS_INPROJ
    tab_spec = pl.BlockSpec((ts, LANE), lambda b, i: (i, 0))
    return pl.pallas_call(
        _inproj_kernel,
        grid=(B, S // ts),
        in_specs=[
            pl.BlockSpec((1, ts, D), lambda b, i: (b, i, 0)),
            _const_spec((1, D)),
            pl.BlockSpec((D, QKVU_W), lambda b, i: (0, 0), pipeline_mode=pl.Buffered(1)),
            tab_spec, tab_spec, tab_spec,
        ],
        out_specs=[
            pl.BlockSpec((1, ts, QKV_W), lambda b, i: (b, i, 0)),
            pl.BlockSpec((1, ts, SSM_W), lambda b, i: (b, i, 0)),
        ],
        out_shape=[
            jax.ShapeDtypeStruct((B, S, QKV_W), BF16),
            jax.ShapeDtypeStruct((B, S, SSM_W), F32),
        ],
        compiler_params=pltpu.CompilerParams(
            dimension_semantics=("arbitrary", "arbitrary"), vmem_limit_bytes=VMEM_LIMIT),
        name="inproj",
    )(x, g, w, *rope)


def _split3(x):
    hi = x.astype(BF16)
    r1 = x - hi.astype(F32)
    mid = r1.astype(BF16)
    lo = (r1 - mid.astype(F32)).astype(BF16)
    return hi, mid, lo


def _head_combine(per_head, lane_lo):
    return jnp.concatenate(
        [jnp.where(lane_lo, per_head[0], per_head[1]), jnp.where(lane_lo, per_head[2], per_head[3])],
        axis=1)


def _attn_block(q, k, v, bias, hm, lane_lo):
    nh = HEADS_PER_GROUP
    qm = jnp.concatenate([q * hm[h:h + 1, :] for h in range(nh)], axis=0)
    s = lax.dot_general(qm, k, (((1,), (1,)), ((), ())), preferred_element_type=F32)
    pv_h, m_h, l_h = [], [], []
    for h in range(nh):
        sh = s[h * BLOCK:(h + 1) * BLOCK] + bias
        m = jnp.max(sh, axis=-1, keepdims=True)
        p = jnp.exp(sh - m)
        l = jnp.sum(p, axis=-1, keepdims=True)
        pv = jnp.dot(p.astype(BF16), v, preferred_element_type=F32)
        pv_h.append(pv[:, (h // 2) * LANE:(h // 2 + 1) * LANE])
        m_h.append(jnp.broadcast_to(m, (BLOCK, LANE)))
        l_h.append(jnp.broadcast_to(l, (BLOCK, LANE)))
    l_b = _head_combine(l_h, lane_lo)
    o = _head_combine(pv_h, lane_lo) * (1.0 / l_b)
    return o, _head_combine(m_h, lane_lo) + jnp.log(l_b)


def _attn_kernel(qkv_ref, hm_ref, pf4_ref, pf16_ref, o_ref, qs, ks, vs, o_scr, l_scr, bias_scr):
    S = qkv_ref.shape[1]
    n_chunks = ATTN_OUT_W // LANE
    perms = {4: pf4_ref, 16: pf16_ref}
    lane_lo = lax.broadcasted_iota(jnp.int32, (BLOCK, LANE), 1) < HEAD_DIM
    qi = lax.broadcasted_iota(jnp.int32, (BLOCK, 2 * BLOCK), 0)
    ki = lax.broadcasted_iota(jnp.int32, (BLOCK, 2 * BLOCK), 1)
    dist = qi + BLOCK - ki
    band = (dist >= 0) & (dist <= BLOCK)
    bias_scr[1] = jnp.where(band, 0.0, NEG_INF)
    bias_scr[0] = jnp.where(band & (ki >= BLOCK), 0.0, NEG_INF)
    hm = hm_ref[...]

    for gi, (window, d) in enumerate(ATTN_GROUPS):
        L = S // d
        nb = L // BLOCK
        cols = [part * N_ATTN_HEADS * HEAD_DIM + gi * ATTN_OUT_W for part in range(3)]
        plen = PERM_ROWS // d
        ppb = BLOCK // plen
        if d > 1:
            pf = perms[d][...]
            for scr, col in zip((qs, ks, vs), cols):
                for mblk in range(S // PERM_ROWS):
                    rs = slice(mblk * PERM_ROWS, (mblk + 1) * PERM_ROWS)
                    blk = qkv_ref[0, rs, col:col + ATTN_OUT_W]
                    scr[rs, :] = jnp.dot(pf, blk, preferred_element_type=F32).astype(BF16)

        def piece_rows(r, n, jj, d=d, plen=plen, ppb=ppb):
            if d == 1:
                return pl.ds(pl.multiple_of(n * BLOCK, BLOCK), BLOCK)
            return pl.ds(pl.multiple_of((n * ppb + jj) * PERM_ROWS + r * plen, plen), plen)

        def load(scr, col, r, n, d=d, ppb=ppb, piece_rows=piece_rows):
            if d == 1:
                return qkv_ref[0, piece_rows(r, n, 0), col:col + ATTN_OUT_W]
            return jnp.concatenate([scr[piece_rows(r, n, jj), :] for jj in range(ppb)], axis=0)

        def body(idx, carry, gi=gi, d=d, nb=nb, cols=cols, load=load, plen=plen, ppb=ppb,
                 piece_rows=piece_rows):
            r = idx // nb
            n = idx % nb
            q = load(qs, cols[0], r, n)
            k = load(ks, cols[1], r, n)
            v = load(vs, cols[2], r, n)
            if nb > 1:
                prev = jnp.maximum(n - 1, 0)
                k = jnp.concatenate([load(ks, cols[1], r, prev), k], axis=0)
                v = jnp.concatenate([load(vs, cols[2], r, prev), v], axis=0)
                bias = bias_scr[jnp.minimum(n, 1)]
            else:
                bias = bias_scr[1, :, BLOCK:]
            o, lse_b = _attn_block(q, k, v, bias, hm, lane_lo)
            start = r + d * BLOCK * n
            rows = pl.ds(pl.multiple_of(start, BLOCK), BLOCK) if d == 1 else pl.ds(start, BLOCK, stride=d)
            for c in range(n_chunks):
                o_scr[gi, c, rows, :] = o[:, c * LANE:(c + 1) * LANE]
                l_scr[gi, c, rows, :] = lse_b[:, c * LANE:(c + 1) * LANE]
            return carry

        lax.fori_loop(0, d * nb, body, 0, unroll=ATTN_UNROLL)

    n_groups = len(ATTN_GROUPS)

    def merge(i, carry):
        rs = pl.ds(pl.multiple_of(i * PERM_ROWS, PERM_ROWS), PERM_ROWS)
        for c in range(n_chunks):
            ls = [l_scr[g, c, rs, :] for g in range(n_groups)]
            m = functools.reduce(jnp.maximum, ls)
            es = [jnp.exp(l - m) for l in ls]
            inv_den = 1.0 / functools.reduce(lambda a, b: a + b, es)
            acc = None
            for g in range(n_groups):
                term = (es[g] * inv_den) * o_scr[g, c, rs, :]
                acc = term if acc is None else acc + term
            o_ref[0, rs, c * LANE:(c + 1) * LANE] = acc.astype(o_ref.dtype)
        return carry

    lax.fori_loop(0, S // PERM_ROWS, merge, 0)


def _residue_permutation(d):
    dst = np.arange(PERM_ROWS)
    plen = PERM_ROWS // d
    src = (dst % plen) * d + dst // plen
    return jnp.asarray(src[:, None] == np.arange(PERM_ROWS)[None, :], dtype=BF16)


def _attn(qkv):
    B, S, _ = qkv.shape
    n_groups = len(ATTN_GROUPS)
    for window, d in ATTN_GROUPS:
        assert window // d == BLOCK and S % (d * BLOCK) == 0
        assert d == 1 or (PERM_ROWS % d == 0 and BLOCK % (PERM_ROWS // d) == 0 and S % PERM_ROWS == 0)
    lane_head = np.arange(ATTN_OUT_W) // HEAD_DIM
    head = np.arange(HEADS_PER_GROUP)
    hm = jnp.asarray(np.where(head[:, None] == lane_head[None, :], HEAD_DIM ** -0.5, 0.0), dtype=BF16)
    n_chunks = ATTN_OUT_W // LANE
    perm_mats = [_residue_permutation(d) for _, d in ATTN_GROUPS if d > 1]
    return pl.pallas_call(
        _attn_kernel,
        grid=(B,),
        in_specs=[pl.BlockSpec((1, S, QKV_W), lambda b: (b, 0, 0)), _const_spec(hm.shape)]
        + [_const_spec(m.shape) for m in perm_mats],
        out_specs=pl.BlockSpec((1, S, ATTN_OUT_W), lambda b: (b, 0, 0)),
        out_shape=jax.ShapeDtypeStruct((B, S, ATTN_OUT_W), BF16),
        scratch_shapes=[
            pltpu.VMEM((S, ATTN_OUT_W), BF16),
            pltpu.VMEM((S, ATTN_OUT_W), BF16),
            pltpu.VMEM((S, ATTN_OUT_W), BF16),
            pltpu.VMEM((n_groups, n_chunks, S, LANE), F32),
            pltpu.VMEM((n_groups, n_chunks, S, LANE), F32),
            pltpu.VMEM((2, BLOCK, 2 * BLOCK), F32),
        ],
        compiler_params=pltpu.CompilerParams(
            dimension_semantics=("arbitrary",), vmem_limit_bytes=VMEM_LIMIT),
        name="attn",
    )(qkv, hm, *perm_mats)


def _ssm_kernel(u_ref, p_ref, pt_ref, wbu_ref, wc_ref, a_ref, d_ref, y_ref,
                bu_scr, x_scr, st_scr, us_scr):
    nb, tc, _ = u_ref.shape
    rows = nb * tc
    n_half = nb // SUBLANE
    hrows = SUBLANE * tc
    n_chunks = SSM_LANES // LANE
    tile_n = 2 * LANE
    assert SSM_PAIR * LANE == tile_n, "one scan pass covers the state lanes of one matmul tile"

    @pl.when(pl.program_id(0) == 0)
    def _():
        st_scr[...] = jnp.zeros_like(st_scr)
        bu_scr[...] = jnp.zeros_like(bu_scr)
        x_scr[...] = jnp.zeros_like(x_scr)
        us_scr[1] = jnp.zeros(us_scr.shape[1:], F32)
        us_scr[2] = jnp.zeros(us_scr.shape[1:], F32)

    us_scr[0] = us_scr[1]
    us_scr[1] = us_scr[2]

    def permute(pm, x):
        return jnp.concatenate(
            [jnp.dot(pm, x[h * hrows:(h + 1) * hrows], preferred_element_type=F32)
             for h in range(n_half)], axis=0)

    u_hi, u_mid, u_lo = _split3(u_ref[...].reshape(rows, SSM_W))
    p_fwd = p_ref[...]
    ut_hi = permute(p_fwd, u_hi)
    us_scr[2] = ut_hi + permute(p_fwd, u_mid) + permute(p_fwd, u_lo)
    ub = ut_hi.astype(BF16)

    out_tile = SSM_OUT_TILE
    k_chunks = (out_tile // SSM_CH * SSM_STATE) // LANE
    cp_per_out = k_chunks // SSM_PAIR
    ys = []

    for cp in range(n_chunks // SSM_PAIR):
        if cp % cp_per_out == 0:
            j = cp // cp_per_out
            acc = None
            for ri in range(2):
                xs = jnp.concatenate(
                    [x_scr[ri, j * k_chunks + c] for c in range(k_chunks)], axis=1)
                part = jnp.dot(xs, wc_ref[ri, j], preferred_element_type=F32)
                acc = part if acc is None else acc + part
            ys.append(acc)
        chains = [(cp * SSM_PAIR + j, h) for j in range(SSM_PAIR) for h in range(n_half)]
        a_re = {c: a_ref[0, c] for c, _ in chains}
        a_im = {c: a_ref[1, c] for c, _ in chains}
        carry = [st_scr[ri, c, h] for c, h in chains for ri in range(2)]
        pack = 4 // x_scr.dtype.itemsize
        for t0 in range(0, tc, pack):
            for j, (c, h) in enumerate(chains):
                xr, xi = carry[2 * j], carry[2 * j + 1]
                outs_r, outs_i = [], []
                for t in range(t0, t0 + pack):
                    row = pl.ds(h * hrows + SUBLANE * t, SUBLANE)
                    xr, xi = (a_re[c] * xr - a_im[c] * xi + bu_scr[0, c, row, :],
                              a_re[c] * xi + a_im[c] * xr + bu_scr[1, c, row, :])
                    outs_r.append(xr)
                    outs_i.append(xi)
                rows_p = pl.ds(h * hrows + SUBLANE * t0, SUBLANE * pack)
                x_scr[0, c, rows_p, :] = jnp.concatenate(outs_r, axis=0).astype(x_scr.dtype)
                x_scr[1, c, rows_p, :] = jnp.concatenate(outs_i, axis=0).astype(x_scr.dtype)
                carry[2 * j], carry[2 * j + 1] = xr, xi
        for j, (c, h) in enumerate(chains):
            st_scr[0, c, h] = carry[2 * j]
            st_scr[1, c, h] = carry[2 * j + 1]
        k0 = (cp * tile_n // SSM_STATE * SSM_CH) // LANE * LANE
        for ri in range(2):
            bu = jnp.dot(ub[:, k0:k0 + LANE], wbu_ref[ri, cp], preferred_element_type=F32)
            for c in range(SSM_PAIR):
                bu_scr[ri, cp * SSM_PAIR + c] = bu[:, c * LANE:(c + 1) * LANE]

    y = jnp.concatenate(ys, axis=1) + d_ref[...] * us_scr[0]
    y = jax.nn.gelu(y).astype(BF16)
    y = permute(pt_ref[...], y).astype(y_ref.dtype)
    y_ref[...] = y.reshape(nb, tc, SSM_W)


def _ssm_params(a_re, a_im, log_dt, b_re, b_im, c_re, c_im, d_skip):
    lr, li = a_re.astype(F32), a_im.astype(F32)
    dt = jnp.exp(log_dt.astype(F32))[:, None]
    mag = jnp.exp(lr * dt)
    ab_re, ab_im = mag * jnp.cos(li * dt), mag * jnp.sin(li * dt)
    den = lr * lr + li * li
    nr, ni = ab_re - 1.0, ab_im
    f_re = (nr * lr + ni * li) / den
    f_im = (ni * lr - nr * li) / den
    br, bi = b_re.astype(F32), b_im.astype(F32)
    bb_re = f_re[..., None] * br - f_im[..., None] * bi
    bb_im = f_re[..., None] * bi + f_im[..., None] * br
    tile_n = 2 * LANE
    n_bu = SSM_LANES // tile_n
    n_c = SSM_W // SSM_OUT_TILE
    k_rows = SSM_OUT_TILE // SSM_CH * SSM_STATE

    def bu_tiles(bb):
        rows = jnp.transpose(bb, (0, 2, 1)).reshape(SSM_W // LANE, LANE, SSM_STATE)
        k_slice = (np.arange(n_bu) * tile_n // SSM_STATE * SSM_CH) // LANE
        tiled = jnp.tile(rows[k_slice], (1, 1, tile_n // SSM_STATE))
        g_row = k_slice[:, None, None] * (LANE // SSM_CH) + np.arange(LANE)[None, :, None] // SSM_CH
        g_col = (np.arange(n_bu)[:, None, None] * tile_n + np.arange(tile_n)[None, None, :]) // SSM_STATE
        return jnp.where(g_row == g_col, tiled, 0.0)

    def c_tiles(c):
        rows = jnp.transpose(c, (0, 2, 1)).reshape(n_c, k_rows, SSM_CH)
        tiled = jnp.tile(rows, (1, 1, SSM_OUT_TILE // SSM_CH))
        same_group = ((np.arange(k_rows)[:, None] // SSM_STATE)
                      == (np.arange(SSM_OUT_TILE)[None, :] // SSM_CH))
        return jnp.where(same_group[None], tiled, 0.0)

    wbu = jnp.stack([bu_tiles(bb_re), bu_tiles(bb_im)]).astype(BF16)
    wc = jnp.stack([c_tiles(c_re.astype(F32)), c_tiles(-c_im.astype(F32))]).astype(BF16)
    n_chunks = SSM_LANES // LANE
    a = jnp.stack([ab_re.reshape(n_chunks, 1, LANE), ab_im.reshape(n_chunks, 1, LANE)])
    a = jnp.broadcast_to(a, (2, n_chunks, SUBLANE, LANE))
    d = d_skip.astype(F32).reshape(1, SSM_W)
    return wbu, wc, a, d


def _ssm_row_permutation(tc):
    dst = np.arange(SUBLANE * tc)
    src = (dst % SUBLANE) * tc + dst // SUBLANE
    p = src[:, None] == np.arange(SUBLANE * tc)[None, :]
    return jnp.asarray(p, dtype=BF16), jnp.asarray(p.T, dtype=BF16)


def _ssm(u, wbu, wc, a, d):
    B, S, _ = u.shape
    tc = SSM_TC
    n_chunks = SSM_LANES // LANE
    assert B % SUBLANE == 0 and S % tc == 0 and tc % 2 == 0
    p_fwd, p_bwd = _ssm_row_permutation(tc)
    n_steps = S // tc
    return pl.pallas_call(
        _ssm_kernel,
        grid=(n_steps + 2,),
        in_specs=[
            pl.BlockSpec((B, tc, SSM_W), lambda i: (0, jnp.minimum(i, n_steps - 1), 0)),
            _const_spec(p_fwd.shape),
            _const_spec(p_bwd.shape),
            _const_spec(wbu.shape),
            _const_spec(wc.shape),
            _const_spec(a.shape),
            _const_spec(d.shape),
        ],
        out_specs=pl.BlockSpec((B, tc, SSM_W), lambda i: (0, jnp.maximum(i - 2, 0), 0)),
        out_shape=jax.ShapeDtypeStruct((B, S, SSM_W), BF16),
        scratch_shapes=[
            pltpu.VMEM((2, n_chunks, B * tc, LANE), F32),
            pltpu.VMEM((2, n_chunks, B * tc, LANE), BF16),
            pltpu.VMEM((2, n_chunks, B // SUBLANE, SUBLANE, LANE), F32),
            pltpu.VMEM((3, B * tc, SSM_W), F32),
        ],
        compiler_params=pltpu.CompilerParams(
            dimension_semantics=("arbitrary",), vmem_limit_bytes=VMEM_LIMIT),
        name="ssm",
    )(u, p_fwd, p_bwd, wbu, wc, a, d)


def _mix_kernel(x_ref, attn_ref, y_ref, g_ref, wg_ref, wao_ref, wglu_ref, wout_ref, o_ref):
    tm = x_ref.shape[1]
    sub = tm // MIX_SPLIT
    for r0 in range(0, tm, sub):
        rs = slice(r0, r0 + sub)
        x = x_ref[0, rs, :]
        h = _rmsnorm(x, g_ref[...]).astype(BF16)
        gate = jax.nn.sigmoid(jnp.dot(h, wg_ref[:, QKVU_W:], preferred_element_type=F32))
        attn_d = jnp.dot(attn_ref[0, rs, :], wao_ref[...], preferred_element_type=F32)
        z = jnp.dot(y_ref[0, rs, :], wglu_ref[...], preferred_element_type=F32)
        ssm_out = z[:, :D_MODEL] * jax.nn.sigmoid(z[:, D_MODEL:])
        merged = gate[:, :D_MODEL] * attn_d + gate[:, D_MODEL:] * ssm_out
        o_ref[0, rs, :] = x + jnp.dot(merged.astype(BF16), wout_ref[...], preferred_element_type=F32)


def _mix(x, attn, y, g, wg, wao, wglu, wout):
    B, S, D = x.shape
    tm = TM_MIX
    tok = lambda w: pl.BlockSpec((1, tm, w), lambda b, i: (b, i, 0))
    return pl.pallas_call(
        _mix_kernel,
        grid=(B, S // tm),
        in_specs=[tok(D), tok(ATTN_OUT_W), tok(SSM_W), _const_spec(g.shape), _const_spec(wg.shape),
                  _const_spec(wao.shape), _const_spec(wglu.shape), _const_spec(wout.shape)],
        out_specs=tok(D),
        out_shape=jax.ShapeDtypeStruct((B, S, D), F32),
        compiler_params=pltpu.CompilerParams(
            dimension_semantics=("arbitrary", "arbitrary"), vmem_limit_bytes=VMEM_LIMIT),
        name="mix",
    )(x, attn, y, g, wg, wao, wglu, wout)


def _ffn_kernel(x_ref, g2_ref, gf_ref, wgate_ref, wup_ref, wdown_ref, o_ref):
    tm = x_ref.shape[1]
    sub = tm // FFN_SPLIT
    for r0 in range(0, tm, sub):
        rs = slice(r0, r0 + sub)
        x = x_ref[0, rs, :]
        h = _rmsnorm(x, g2_ref[...]).astype(BF16)
        gate = jnp.dot(h, wgate_ref[...], preferred_element_type=F32)
        up = jnp.dot(h, wup_ref[...], preferred_element_type=F32)
        act = (jax.nn.silu(gate) * up).astype(BF16)
        ff = jnp.dot(act, wdown_ref[...], preferred_element_type=F32)
        o_ref[0, rs, :] = _rmsnorm(x + ff, gf_ref[...])


def _ffn(x, g2, gf, wgate, wup, wdown):
    B, S, D = x.shape
    tm = TM_FFN
    tok = pl.BlockSpec((1, tm, D), lambda b, i: (b, i, 0))
    return pl.pallas_call(
        _ffn_kernel,
        grid=(B, S // tm),
        in_specs=[tok, _const_spec(g2.shape), _const_spec(gf.shape), _const_spec(wgate.shape),
                  _const_spec(wup.shape), _const_spec(wdown.shape)],
        out_specs=tok,
        out_shape=jax.ShapeDtypeStruct((B, S, D), F32),
        compiler_params=pltpu.CompilerParams(
            dimension_semantics=("arbitrary", "arbitrary"), vmem_limit_bytes=VMEM_LIMIT),
        name="ffn",
    )(x, g2, gf, wgate, wup, wdown)


def _mixffn_kernel(x_ref, attn_ref, y_ref, g_ref, wg_ref, wao_ref, wglu_ref, wout_ref,
                   g2_ref, gf_ref, wgate_ref, wup_ref, wdown_ref, o_ref):
    tm = x_ref.shape[1]
    sub = tm // MIXFFN_SPLIT
    for r0 in range(0, tm, sub):
        rs = slice(r0, r0 + sub)
        x = x_ref[0, rs, :]
        h = _rmsnorm(x, g_ref[...]).astype(BF16)
        gate = jax.nn.sigmoid(jnp.dot(h, wg_ref[...], preferred_element_type=F32))
        attn_d = jnp.dot(attn_ref[0, rs, :], wao_ref[...], preferred_element_type=F32)
        z = jnp.dot(y_ref[0, rs, :], wglu_ref[...], preferred_element_type=F32)
        ssm_out = z[:, :D_MODEL] * jax.nn.sigmoid(z[:, D_MODEL:])
        merged = gate[:, :D_MODEL] * attn_d + gate[:, D_MODEL:] * ssm_out
        x = x + jnp.dot(merged.astype(BF16), wout_ref[...], preferred_element_type=F32)
        h = _rmsnorm(x, g2_ref[...]).astype(BF16)
        ffn_gate = jnp.dot(h, wgate_ref[...], preferred_element_type=F32)
        up = jnp.dot(h, wup_ref[...], preferred_element_type=F32)
        act = (jax.nn.silu(ffn_gate) * up).astype(BF16)
        ff = jnp.dot(act, wdown_ref[...], preferred_element_type=F32)
        o_ref[0, rs, :] = _rmsnorm(x + ff, gf_ref[...])


def _mixffn(x, attn, y, g, wg, wao, wglu, wout, g2, gf, wgate, wup, wdown):
    B, S, D = x.shape
    tm = TM_MIXFFN
    tok = lambda w: pl.BlockSpec((1, tm, w), lambda b, i: (b, i, 0))
    weights = (g, wg, wao, wglu, wout, g2, gf, wgate, wup, wdown)
    return pl.pallas_call(
        _mixffn_kernel,
        grid=(B, S // tm),
        in_specs=[tok(D), tok(ATTN_OUT_W), tok(SSM_W)] + [_const_spec(w.shape) for w in weights],
        out_specs=tok(D),
        out_shape=jax.ShapeDtypeStruct((B, S, D), F32),
        compiler_params=pltpu.CompilerParams(
            dimension_semantics=("arbitrary", "arbitrary"), vmem_limit_bytes=VMEM_LIMIT),
        name="mixffn",
    )(x, attn, y, *weights)


def kernel(x, norm_mix_g, w_in, ssm_a_re, ssm_a_im, ssm_log_dt, ssm_b_re, ssm_b_im, ssm_c_re, ssm_c_im, ssm_d, w_glu, w_attn_out, w_out, norm_ffn_g, w_ffn_gate, w_ffn_up, w_ffn_down, norm_final_g):
    B, S, D = x.shape
    assert D == D_MODEL and norm_mix_g.shape[0] == 1, "single-layer block expected"
    layer = 0
    g_mix = norm_mix_g[layer].reshape(1, D).astype(F32)
    assert w_in.shape[2] == QKVU_W + 2 * D
    w_in_b = w_in[layer].astype(BF16)

    qkv, u = _inproj(x, g_mix, w_in_b, _rope_tables(S))
    attn = _attn(qkv)
    wbu, wc, a, d = _ssm_params(
        ssm_a_re[layer], ssm_a_im[layer], ssm_log_dt[layer], ssm_b_re[layer], ssm_b_im[layer],
        ssm_c_re[layer], ssm_c_im[layer], ssm_d[layer])
    y = _ssm(u, wbu, wc, a, d)
    return _mixffn(x, attn, y, g_mix, w_in_b[:, QKVU_W:], w_attn_out[layer].astype(BF16),
                   w_glu[layer].astype(BF16), w_out[layer].astype(BF16),
                   norm_ffn_g[layer].reshape(1, D).astype(F32),
                   norm_final_g.reshape(1, D).astype(F32), w_ffn_gate[layer].astype(BF16),
                   w_ffn_up[layer].astype(BF16), w_ffn_down[layer].astype(BF16))
```
